```python
import jax, jax.numpy as jnp
from jax import lax
import numpy as np

D_MODEL = 1024
BATCH = 4
SEQ = 4096
DEPTH = 1
DEC_BATCH = 16
DEC_SEQ = 2048
PAST_LEN = 128

HEAD_DIM = 64
A_Q_HEADS = 8
A_KV_HEADS = 2
A_GROUP = A_Q_HEADS // A_KV_HEADS
B_WINDOWS = (128, 512, 2048)
B_DILATIONS = (1, 4, 16)
B_N_GROUPS = len(B_WINDOWS)
B_HEADS_PER_GROUP = 4
B_HEADS = B_N_GROUPS * B_HEADS_PER_GROUP
D_FF = 4 * D_MODEL
GRID_W = 64
AXIAL_THETA = 10000.0
PARTIAL_THETA = 500000.0
PARTIAL_ROPE_DIM = HEAD_DIM // 4
Q_BLOCK = 128
EPS = 1e-6

A_Q_W = A_Q_HEADS * HEAD_DIM
A_KV_W = A_KV_HEADS * HEAD_DIM
B_W = B_HEADS * HEAD_DIM
B_OUT_W = B_HEADS_PER_GROUP * HEAD_DIM
IN_SPLITS = (A_Q_W, A_KV_W, A_KV_W, B_W, B_W, B_W, D_MODEL, D_MODEL)
IN_W = sum(IN_SPLITS)

kernel_name = "hybrid_axial_gqa_dilated_encoder"


def rmsnorm(x, g):
    xf = x.astype(jnp.float32)
    y = xf * lax.rsqrt(jnp.mean(xf * xf, axis=-1, keepdims=True) + EPS)
    return (y * g.astype(jnp.float32)).astype(x.dtype)


def rope(x, pos, theta):
    d = x.shape[-1]
    d2 = d // 2
    freqs = theta ** (-(jnp.arange(d2, dtype=jnp.float32) * 2.0) / d)
    ang = pos.astype(jnp.float32)[:, None] * freqs[None, :]
    bshape = (pos.shape[0],) + (1,) * (x.ndim - 3) + (d2,)
    cos = jnp.cos(ang).reshape(bshape)
    sin = jnp.sin(ang).reshape(bshape)
    xf = x.astype(jnp.float32)
    x1, x2 = xf[..., :d2], xf[..., d2:]
    return jnp.concatenate([x1 * cos - x2 * sin, x2 * cos + x1 * sin], axis=-1).astype(x.dtype)


def axial_gqa(q, k, v, q_gain, k_gain):
    Bn, S = q.shape[0], q.shape[1]
    rows = S // GRID_W
    r, c = jnp.meshgrid(jnp.arange(rows), jnp.arange(GRID_W), indexing="ij")
    r = r.reshape(-1)
    c = c.reshape(-1)
    q = rmsnorm(q, q_gain)
    k = rmsnorm(k, k_gain)
    half = HEAD_DIM // 2

    def axial(t):
        return jnp.concatenate([rope(t[..., :half], r, AXIAL_THETA),
                                rope(t[..., half:], c, AXIAL_THETA)], axis=-1)

    q = axial(q)
    k = axial(k)
    nb = S // Q_BLOCK
    qb = q.reshape(Bn, nb, Q_BLOCK, A_KV_HEADS, A_GROUP, HEAD_DIM).transpose(1, 0, 2, 3, 4, 5)
    scale = HEAD_DIM ** -0.5

    def block(qi):
        s = jnp.einsum("bqkgd,bskd->bkgqs", qi, k).astype(jnp.float32) * scale
        p = jax.nn.softmax(s, axis=-1)
        return jnp.einsum("bkgqs,bskd->bqkgd", p.astype(v.dtype), v)

    o = lax.map(block, qb)
    return o.transpose(1, 0, 2, 3, 4, 5).reshape(Bn, S, A_Q_W)


def dilated_attention(q, k, v):
    Bn, S = q.shape[0], q.shape[1]
    t = jnp.arange(S)

    def prope(x):
        return jnp.concatenate([rope(x[..., :PARTIAL_ROPE_DIM], t, PARTIAL_THETA),
                                x[..., PARTIAL_ROPE_DIM:]], axis=-1)

    q = prope(q)
    k = prope(k)
    offs = jnp.stack([jnp.arange(-(w // 2) // d, (w // 2) // d + 1) * d
                      for w, d in zip(B_WINDOWS, B_DILATIONS)])
    nb = S // Q_BLOCK
    qb = q.reshape(Bn, nb, Q_BLOCK, B_N_GROUPS, B_HEADS_PER_GROUP, HEAD_DIM).transpose(1, 0, 2, 3, 4, 5)
    starts = jnp.arange(nb) * Q_BLOCK
    g_idx = jnp.arange(B_N_GROUPS)[None, :, None]
    scale = HEAD_DIM ** -0.5

    def block(args):
        qi, s0 = args
        idx = s0 + jnp.arange(Q_BLOCK)[:, None, None] + offs[None]
        valid = (idx >= 0) & (idx < S)
        idxc = jnp.clip(idx, 0, S - 1)
        kg = k[:, idxc, g_idx]
        vg = v[:, idxc, g_idx]
        s = jnp.einsum("bqghd,bqgjhd->bqghj", qi, kg).astype(jnp.float32) * scale
        s = jnp.where(valid[None, :, :, None, :], s, -jnp.inf)
        m = jnp.max(s, axis=-1, keepdims=True)
        p = jnp.exp(s - m)
        den = jnp.sum(p, axis=-1, keepdims=True)
        o = jnp.einsum("bqghj,bqgjhd->bqghd", (p / den).astype(v.dtype), vg)
        lse = (m + jnp.log(den))[..., 0]
        w = jax.nn.softmax(lse, axis=2)
        return jnp.einsum("bqgh,bqghd->bqhd", w.astype(o.dtype), o)

    o = lax.map(block, (qb, starts))
    return o.transpose(1, 0, 2, 3, 4).reshape(Bn, S, B_OUT_W)


def layer(x, w_in, w_a_out, w_b_out, w_out, g_mix, q_gain, k_gain, g_mlp, w_up, w_down):
    Bn, S, _ = x.shape
    h = rmsnorm(x, g_mix)
    z = h @ w_in
    cuts = [int(c) for c in np.cumsum(IN_SPLITS)[:-1]]
    qa, ka, va, qb, kb, vb, ga, gb = jnp.split(z, cuts, axis=-1)
    ya = axial_gqa(qa.reshape(Bn, S, A_Q_HEADS, HEAD_DIM),
                   ka.reshape(Bn, S, A_KV_HEADS, HEAD_DIM),
                   va.reshape(Bn, S, A_KV_HEADS, HEAD_DIM), q_gain, k_gain) @ w_a_out
    bshape = (Bn, S, B_N_GROUPS, B_HEADS_PER_GROUP, HEAD_DIM)
    yb = dilated_attention(qb.reshape(bshape), kb.reshape(bshape), vb.reshape(bshape)) @ w_b_out
    mixed = jax.nn.sigmoid(ga) * ya + jax.nn.sigmoid(gb) * yb
    x = x + mixed @ w_out
    h = rmsnorm(x, g_mlp)
    u = jnp.square(jax.nn.relu(h @ w_up))
    return x + u @ w_down


def trunk(x, w_in, w_a_out, w_b_out, w_out, g_mix, q_gain, k_gain, g_mlp, w_up, w_down, g_final):
    for l in range(DEPTH):
        x = layer(x, w_in[l], w_a_out[l], w_b_out[l], w_out[l], g_mix[l], q_gain[l], k_gain[l],
                  g_mlp[l], w_up[l], w_down[l])
    return rmsnorm(x, g_final)


def setup_inputs(seed: int = 0) -> dict:
    key = jax.random.key(seed)
    ks = jax.random.split(key, 13)
    f32 = jnp.float32

    def w(k, shape, fan_in):
        return jax.random.normal(k, shape, f32) * (fan_in ** -0.5)

    def gain(k, shape):
        return 1.0 + 0.02 * jax.random.normal(k, shape, f32)

    return {
        "x_prompt": jax.random.normal(ks[0], (BATCH, SEQ, D_MODEL), f32),
        "x_sample": jax.random.normal(ks[1], (DEC_BATCH, DEC_SEQ, D_MODEL), f32),
        "w_in": w(ks[2], (DEPTH, D_MODEL, IN_W), D_MODEL),
        "w_a_out": w(ks[3], (DEPTH, A_Q_W, D_MODEL), A_Q_W),
        "w_b_out": w(ks[4], (DEPTH, B_OUT_W, D_MODEL), B_OUT_W),
        "w_out": w(ks[5], (DEPTH, D_MODEL, D_MODEL), D_MODEL),
        "g_mix": gain(ks[6], (DEPTH, D_MODEL)),
        "q_gain": gain(ks[7], (DEPTH, HEAD_DIM)),
        "k_gain": gain(ks[8], (DEPTH, HEAD_DIM)),
        "g_mlp": gain(ks[9], (DEPTH, D_MODEL)),
        "w_up": w(ks[10], (DEPTH, D_MODEL, D_FF), D_MODEL),
        "w_down": w(ks[11], (DEPTH, D_FF, D_MODEL), D_FF),
        "g_final": gain(ks[12], (D_MODEL,)),
    }


def reference(x_prompt, x_sample, w_in, w_a_out, w_b_out, w_out, g_mix, q_gain, k_gain,
              g_mlp, w_up, w_down, g_final):
    y_prompt = trunk(x_prompt, w_in, w_a_out, w_b_out, w_out, g_mix, q_gain, k_gain,
                     g_mlp, w_up, w_down, g_final)
    y_sample = trunk(x_sample, w_in, w_a_out, w_b_out, w_out, g_mix, q_gain, k_gain,
                     g_mlp, w_up, w_down, g_final)
    return (y_prompt, y_sample)
```

```python
import functools
import math

import jax
import jax.numpy as jnp
from jax import lax
from jax.experimental import pallas as pl
from jax.experimental.pallas import tpu as pltpu

F32 = jnp.float32
BF16 = jnp.bfloat16

HEAD_DIM = 64
A_Q_HEADS = 8
A_KV_HEADS = 2
B_DILATIONS = (1, 4, 16)
B_HALF_WINDOW = 64
B_HEADS_PER_GROUP = 4
GRID_W = 64
AXIAL_THETA = 10000.0
PARTIAL_THETA = 500000.0
PARTIAL_ROPE_DIM = HEAD_DIM // 4
EPS = 1e-6
MASK_VALUE = -1e30

LANES = 128
A_Q_W = A_Q_HEADS * HEAD_DIM
A_KV_W = A_KV_HEADS * HEAD_DIM
B_GROUP_W = B_HEADS_PER_GROUP * HEAD_DIM
B_W = len(B_DILATIONS) * B_GROUP_W
Q_SCALE = HEAD_DIM ** -0.5 * math.log2(math.e)

VMEM_LIMIT = 56 * 1024 * 1024


def _const_spec(shape):
    return pl.BlockSpec(shape, lambda *_: (0,) * len(shape), pipeline_mode=pl.Buffered(1))


def _rope_tables(seq):
    t = jnp.arange(seq)

    def half_tables(pos, d, theta):
        d2 = d // 2
        freqs = theta ** (-(jnp.arange(d2, dtype=F32) * 2.0) / d)
        ang = pos.astype(F32)[:, None] * freqs[None, :]
        cos, sin = jnp.cos(ang), jnp.sin(ang)
        zero = jnp.zeros_like(sin)
        return (jnp.concatenate([cos, cos], -1), jnp.concatenate([-sin, zero], -1),
                jnp.concatenate([zero, sin], -1))

    half = HEAD_DIM // 2
    row = half_tables(t // GRID_W, half, AXIAL_THETA)
    col = half_tables(t % GRID_W, half, AXIAL_THETA)
    axial = [jnp.concatenate([r, c], -1) for r, c in zip(row, col)]
    part = half_tables(t, PARTIAL_ROPE_DIM, PARTIAL_THETA)
    rest = HEAD_DIM - PARTIAL_ROPE_DIM
    fill = (jnp.ones((seq, rest), F32), jnp.zeros((seq, rest), F32), jnp.zeros((seq, rest), F32))
    partial = [jnp.concatenate([p, f], -1) for p, f in zip(part, fill)]
    reps = LANES // HEAD_DIM
    tab_a = jnp.stack([jnp.tile(a, (1, reps)) for a in axial])
    tab_b = jnp.stack([jnp.tile(p, (1, reps)) for p in partial])
    return tab_a, tab_b


def _inproj_kernel(x_ref, gmix_ref, w_ref, qg_ref, kg_ref, ta_ref, tbq_ref, tbk_ref,
                   qa_ref, ka_ref, va_ref, b0_ref, b1_ref, b2_ref, gate_ref, dl_ref, *, tn):
    x = x_ref[0]
    ms = jnp.mean(x * x, axis=-1, keepdims=True)
    h = (x * lax.rsqrt(ms + EPS) * gmix_ref[...]).astype(BF16)

    lane = lax.broadcasted_iota(jnp.int32, (tn, LANES), 1)
    low = lane < HEAD_DIM
    rr = lax.broadcasted_iota(jnp.int32, (LANES, LANES), 0) // HEAD_DIM
    cc = lax.broadcasted_iota(jnp.int32, (LANES, LANES), 1) // HEAD_DIM
    seg_ones = jnp.where(rr == cc, 1.0, 0.0).astype(BF16)

    def proj(c0, c1):
        return jnp.dot(h, w_ref[:, c0:c1], preferred_element_type=F32)

    def head_norm(z, gain):
        z2 = z * z
        hi = z2.astype(BF16)
        lo = (z2 - hi.astype(F32)).astype(BF16)
        ss = (jnp.dot(hi, seg_ones, preferred_element_type=F32)
              + jnp.dot(lo, seg_ones, preferred_element_type=F32))
        return z * lax.rsqrt(ss * (1.0 / HEAD_DIM) + EPS) * gain

    def rope(z, tab_ref, sh):
        return (z * tab_ref[0] + pltpu.roll(z, LANES - sh, 1) * tab_ref[1]
                + pltpu.roll(z, sh, 1) * tab_ref[2])

    def pair_rep(z):
        zr = pltpu.roll(z, HEAD_DIM, 1)
        return jnp.where(low, z, zr), jnp.where(low, zr, z)

    a_sh = HEAD_DIM // 4
    for blk in range(A_Q_W // LANES):
        z = proj(blk * LANES, (blk + 1) * LANES)
        z = rope(head_norm(z, qg_ref[...]), ta_ref, a_sh)
        qa_ref[0, :, blk * LANES:(blk + 1) * LANES] = z.astype(BF16)
    zk = rope(head_norm(proj(A_Q_W, A_Q_W + A_KV_W), kg_ref[...]), ta_ref, a_sh)
    k0, k1 = pair_rep(zk)
    ka_ref[0, 0] = k0.astype(BF16)
    ka_ref[0, 1] = k1.astype(BF16)
    v0, v1 = pair_rep(proj(A_Q_W + A_KV_W, A_Q_W + 2 * A_KV_W))
    va_ref[0, 0] = v0.astype(BF16)
    va_ref[0, 1] = v1.astype(BF16)

    b_base = A_Q_W + 2 * A_KV_W
    b_refs = (b0_ref, b1_ref, b2_ref)
    b_sh = PARTIAL_ROPE_DIM // 2
    for t, tab_ref in enumerate((tbq_ref, tbk_ref, None)):
        for g, d in enumerate(B_DILATIONS):
            c0 = b_base + t * B_W + g * B_GROUP_W
            z = proj(c0, c0 + B_GROUP_W)
            halves = []
            for s in range(B_GROUP_W // LANES):
                zs = z[:, s * LANES:(s + 1) * LANES]
                halves.append(zs if tab_ref is None else rope(zs, tab_ref, b_sh))
            col = t * B_GROUP_W
            if d == 1:
                for s, zs in enumerate(halves):
                    b_refs[g][0, 0, :, col + s * LANES:col + (s + 1) * LANES] = zs.astype(BF16)
            else:
                for s, zs in enumerate(halves):
                    dl_ref[s] = zs
                for r in range(d):
                    for s in range(len(halves)):
                        piece = dl_ref[s, pl.ds(r, tn // d, stride=d), :]
                        b_refs[g][0, r, :, col + s * LANES:col + (s + 1) * LANES] = piece.astype(BF16)

    g_base = b_base + 3 * B_W
    n_gate = gate_ref.shape[-1]
    for c in range(n_gate // 256):
        z = proj(g_base + c * 256, g_base + (c + 1) * 256)
        gate_ref[0, :, c * 256:(c + 1) * 256] = z.astype(BF16)


def _inproj(x, w_in, g_mix, q_gain, k_gain, tab_a, tab_bq, tab_bk, *, tn):
    bn, seq, dm = x.shape
    in_w = w_in.shape[1]
    gate_w = in_w - (A_Q_W + 2 * A_KV_W + 3 * B_W)
    grid = (bn, seq // tn)
    tab_spec = pl.BlockSpec((3, tn, LANES), lambda b, i: (0, i, 0))
    out_shape = (
        jax.ShapeDtypeStruct((bn, seq, A_Q_W), BF16),
        jax.ShapeDtypeStruct((bn, A_KV_HEADS, seq, LANES), BF16),
        jax.ShapeDtypeStruct((bn, A_KV_HEADS, seq, LANES), BF16),
    ) + tuple(jax.ShapeDtypeStruct((bn, d, seq // d, 3 * B_GROUP_W), BF16) for d in B_DILATIONS) + (
        jax.ShapeDtypeStruct((bn, seq, gate_w), BF16),
    )
    out_specs = (
        pl.BlockSpec((1, tn, A_Q_W), lambda b, i: (b, i, 0)),
        pl.BlockSpec((1, A_KV_HEADS, tn, LANES), lambda b, i: (b, 0, i, 0)),
        pl.BlockSpec((1, A_KV_HEADS, tn, LANES), lambda b, i: (b, 0, i, 0)),
    ) + tuple(pl.BlockSpec((1, d, tn // d, 3 * B_GROUP_W), lambda b, i: (b, 0, i, 0))
              for d in B_DILATIONS) + (
        pl.BlockSpec((1, tn, gate_w), lambda b, i: (b, i, 0)),
    )
    return pl.pallas_call(
        functools.partial(_inproj_kernel, tn=tn),
        grid=grid,
        in_specs=[
            pl.BlockSpec((1, tn, dm), lambda b, i: (b, i, 0)),
            _const_spec((1, dm)),
            _const_spec((dm, in_w)),
            _const_spec((1, LANES)),
            _const_spec((1, LANES)),
            tab_spec, tab_spec, tab_spec,
        ],
        out_specs=out_specs,
        out_shape=out_shape,
        scratch_shapes=[pltpu.VMEM((B_GROUP_W // LANES, tn, LANES), F32)],
        compiler_params=pltpu.CompilerParams(
            dimension_semantics=("parallel", "parallel"), vmem_limit_bytes=VMEM_LIMIT),
        name="inproj",
    )(x, g_mix, w_in, q_gain, k_gain, tab_a, tab_bq, tab_bk)


def _attn_a_kernel(q_ref, k_ref, v_ref, o_ref, qs_ref, m_ref, l_ref, acc_ref, *, tq, tk, nk):
    group = A_Q_HEADS // A_KV_HEADS
    lane = lax.broadcasted_iota(jnp.int32, (tq, LANES), 1)
    low = lane < HEAD_DIM
    for h in range(group):
        pair = q_ref[0, :, (h // 2) * LANES:(h // 2 + 1) * LANES].astype(F32)
        keep = low if h % 2 == 0 else jnp.logical_not(low)
        qs_ref[h * tq:(h + 1) * tq, :] = jnp.where(keep, pair, 0.0).astype(BF16)
    m_ref[...] = jnp.full(m_ref.shape, MASK_VALUE, F32)
    l_ref[...] = jnp.zeros(l_ref.shape, F32)
    acc_ref[...] = jnp.zeros(acc_ref.shape, F32)

    def body(kc, carry):
        off = pl.multiple_of(kc * tk, tk)
        k = k_ref[0, 0, pl.ds(off, tk), :]
        v = v_ref[0, 0, pl.ds(off, tk), :]
        s = lax.dot_general(qs_ref[...], k, (((1,), (1,)), ((), ())),
                            preferred_element_type=F32)
        m_prev = m_ref[...]
        m_next = jnp.maximum(m_prev, jnp.max(s, axis=1, keepdims=True))
        alpha = jnp.exp2(m_prev - m_next)
        p = jnp.concatenate(
            [jnp.exp2(s[:, c * LANES:(c + 1) * LANES] - m_next) for c in range(tk // LANES)], axis=1)
        l_ref[...] = alpha * l_ref[...] + jnp.sum(p, axis=1, keepdims=True)
        m_ref[...] = m_next
        acc_ref[...] = alpha * acc_ref[...] + jnp.dot(p.astype(BF16), v, preferred_element_type=F32)
        return carry

    lax.fori_loop(0, nk, body, 0)
    o = acc_ref[...] / l_ref[...]
    for pr in range(group // 2):
        even = o[(2 * pr) * tq:(2 * pr + 1) * tq]
        odd = o[(2 * pr + 1) * tq:(2 * pr + 2) * tq]
        o_ref[0, :, pr * LANES:(pr + 1) * LANES] = jnp.where(low, even, odd).astype(BF16)


def _attn_a(qa, ka, va, *, tq, tk):
    bn, seq, _ = qa.shape
    group = A_Q_HEADS // A_KV_HEADS
    gw = group * HEAD_DIM
    grid = (bn, A_KV_HEADS, seq // tq)
    kv_spec = pl.BlockSpec((1, 1, seq, LANES), lambda b, j, i: (b, j, 0, 0))
    return pl.pallas_call(
        functools.partial(_attn_a_kernel, tq=tq, tk=tk, nk=seq // tk),
        grid=grid,
        in_specs=[pl.BlockSpec((1, tq, gw), lambda b, j, i: (b, i, j)), kv_spec, kv_spec],
        out_specs=pl.BlockSpec((1, tq, gw), lambda b, j, i: (b, i, j)),
        out_shape=jax.ShapeDtypeStruct((bn, seq, A_Q_W), BF16),
        scratch_shapes=[
            pltpu.VMEM((group * tq, LANES), BF16),
            pltpu.VMEM((group * tq, LANES), F32),
            pltpu.VMEM((group * tq, LANES), F32),
            pltpu.VMEM((group * tq, LANES), F32),
        ],
        compiler_params=pltpu.CompilerParams(
            dimension_semantics=("parallel", "parallel", "parallel"), vmem_limit_bytes=VMEM_LIMIT),
        name="attn_a",
    )(qa, ka, va)


def _attn_b_kernel(q_ref, k_ref, v_ref, o_ref, lse_ref, *, sub_len, tq, tk):
    nh = B_HEADS_PER_GROUP
    blk = lax.broadcasted_iota(jnp.int32, (tq, B_GROUP_W), 1) // HEAD_DIM
    row = lax.broadcasted_iota(jnp.int32, (tq, tk), 0)
    colj = lax.broadcasted_iota(jnp.int32, (tq, tk), 1)

    def tile(t, carry):
        i0 = pl.multiple_of(t * tq, tq)
        w0 = pl.multiple_of(jnp.clip(i0 - B_HALF_WINDOW, 0, sub_len - tk), B_HALF_WINDOW)
        qf = q_ref[0, 0, pl.ds(i0, tq), :].astype(F32)
        k = k_ref[0, 0, pl.ds(w0, tk), :]
        v = v_ref[0, 0, pl.ds(w0, tk), :]
        qs = jnp.concatenate([jnp.where(blk == h, qf, 0.0) for h in range(nh)], axis=0).astype(BF16)
        s = lax.dot_general(qs, k, (((1,), (1,)), ((), ())), preferred_element_type=F32)
        delta = (row + i0) - (colj + w0)
        bias = jnp.where(jnp.abs(delta) <= B_HALF_WINDOW, 0.0, MASK_VALUE)
        s = s + jnp.concatenate([bias] * nh, axis=0)
        m = jnp.max(s, axis=1, keepdims=True)
        p = jnp.exp2(s - m)
        l = jnp.sum(p, axis=1, keepdims=True)
        pv = jnp.dot(p.astype(BF16), v, preferred_element_type=F32)
        o = pv / l
        lse = m + jnp.log2(l)
        out = jnp.zeros((tq, B_GROUP_W), F32)
        lse_out = jnp.zeros((tq, B_GROUP_W), F32)
        for h in range(nh):
            sel = blk == h
            out = jnp.where(sel, o[h * tq:(h + 1) * tq], out)
            lse_out = jnp.where(sel, lse[h * tq:(h + 1) * tq], lse_out)
        o_ref[0, 0, pl.ds(i0, tq), :] = out.astype(BF16)
        lse_ref[0, 0, pl.ds(i0, tq), :] = lse_out
        return carry

    lax.fori_loop(0, sub_len // tq, tile, 0)


def _attn_b(qkv, *, tq=128):
    bn, d, sub_len, _ = qkv.shape
    tq = min(tq, sub_len)
    tk = min(tq + 2 * B_HALF_WINDOW, sub_len)

    def spec(c):
        return pl.BlockSpec((1, 1, sub_len, B_GROUP_W), lambda b, r: (b, r, 0, c))

    out_spec = pl.BlockSpec((1, 1, sub_len, B_GROUP_W), lambda b, r: (b, r, 0, 0))
    return pl.pallas_call(
        functools.partial(_attn_b_kernel, sub_len=sub_len, tq=tq, tk=tk),
        grid=(bn, d),
        in_specs=[spec(0), spec(1), spec(2)],
        out_specs=(out_spec, out_spec),
        out_shape=(jax.ShapeDtypeStruct((bn, d, sub_len, B_GROUP_W), BF16),
                   jax.ShapeDtypeStruct((bn, d, sub_len, B_GROUP_W), F32)),
        compiler_params=pltpu.CompilerParams(
            dimension_semantics=("parallel", "parallel"), vmem_limit_bytes=VMEM_LIMIT),
        name="attn_b",
    )(qkv, qkv, qkv)


def _post_kernel(x_ref, ya_ref, o0_ref, l0_ref, o1_ref, l1_ref, o2_ref, l2_ref, gate_ref,
                 wa_ref, wb_ref, wo_ref, gmlp_ref, wup_ref, wdn_ref, gfin_ref,
                 y_ref, ril_ref, *, tn, ff_chunk, final_norm):
    dm = x_ref.shape[-1]
    n_slab = B_GROUP_W // LANES

    def natural_order(src_ref, d, slot):
        for r in range(d):
            blk = src_ref[0, r].astype(F32)
            for s in range(n_slab):
                ril_ref[slot * n_slab + s, pl.ds(r, tn // d, stride=d), :] = blk[:, s * LANES:(s + 1) * LANES]
        return jnp.concatenate([ril_ref[slot * n_slab + s] for s in range(n_slab)], axis=1)

    o0 = o0_ref[0, 0].astype(F32)
    l0 = l0_ref[0, 0]
    o1 = natural_order(o1_ref, B_DILATIONS[1], 0)
    l1 = natural_order(l1_ref, B_DILATIONS[1], 1)
    o2 = natural_order(o2_ref, B_DILATIONS[2], 2)
    l2 = natural_order(l2_ref, B_DILATIONS[2], 3)
    mx = jnp.maximum(jnp.maximum(l0, l1), l2)
    e0, e1, e2 = jnp.exp2(l0 - mx), jnp.exp2(l1 - mx), jnp.exp2(l2 - mx)
    yb_in = (e0 * o0 + e1 * o1 + e2 * o2) / (e0 + e1 + e2)

    ya = jnp.dot(ya_ref[0], wa_ref[...], preferred_element_type=F32)
    yb = jnp.dot(yb_in.astype(BF16), wb_ref[...], preferred_element_type=F32)
    ga = gate_ref[0, :, :dm].astype(F32)
    gb = gate_ref[0, :, dm:].astype(F32)
    mixed = ya / (1.0 + jnp.exp(-ga)) + yb / (1.0 + jnp.exp(-gb))
    x1 = x_ref[0] + jnp.dot(mixed.astype(BF16), wo_ref[...], preferred_element_type=F32)

    ms = jnp.mean(x1 * x1, axis=-1, keepdims=True)
    h2 = (x1 * lax.rsqrt(ms + EPS) * gmlp_ref[...]).astype(BF16)
    acc = x1
    d_ff = wup_ref.shape[1]
    for c in range(d_ff // ff_chunk):
        u = jnp.dot(h2, wup_ref[:, c * ff_chunk:(c + 1) * ff_chunk], preferred_element_type=F32)
        u = jnp.square(jnp.maximum(u, 0.0)).astype(BF16)
        acc = acc + jnp.dot(u, wdn_ref[c * ff_chunk:(c + 1) * ff_chunk, :], preferred_element_type=F32)
    if final_norm:
        ms = jnp.mean(acc * acc, axis=-1, keepdims=True)
        acc = acc * lax.rsqrt(ms + EPS) * gfin_ref[...]
    y_ref[0] = acc


def _post(x, ya, b_outs, gates, w_a_out, w_b_out, w_out, g_mlp, w_up, w_down, g_final,
          *, tn, final_norm, ff_chunk=512):
    bn, seq, dm = x.shape
    d_ff = w_up.shape[1]
    grid = (bn, seq // tn)
    in_specs = [
        pl.BlockSpec((1, tn, dm), lambda b, i: (b, i, 0)),
        pl.BlockSpec((1, tn, A_Q_W), lambda b, i: (b, i, 0)),
    ]
    operands = [x, ya]
    for d, (o, lse) in zip(B_DILATIONS, b_outs):
        spec = pl.BlockSpec((1, d, tn // d, B_GROUP_W), lambda b, i: (b, 0, i, 0))
        in_specs += [spec, spec]
        operands += [o, lse]
    in_specs += [
        pl.BlockSpec((1, tn, gates.shape[-1]), lambda b, i: (b, i, 0)),
        _const_spec(w_a_out.shape), _const_spec(w_b_out.shape), _const_spec(w_out.shape),
        _const_spec((1, dm)), _const_spec(w_up.shape), _const_spec(w_down.shape), _const_spec((1, dm)),
    ]
    operands += [gates, w_a_out, w_b_out, w_out, g_mlp, w_up, w_down, g_final]
    n_ril = 4 * (B_GROUP_W // LANES)
    return pl.pallas_call(
        functools.partial(_post_kernel, tn=tn, ff_chunk=ff_chunk, final_norm=final_norm),
        grid=grid,
        in_specs=in_specs,
        out_specs=pl.BlockSpec((1, tn, dm), lambda b, i: (b, i, 0)),
        out_shape=jax.ShapeDtypeStruct((bn, seq, dm), F32),
        scratch_shapes=[pltpu.VMEM((n_ril, tn, LANES), F32)],
        compiler_params=pltpu.CompilerParams(
            dimension_semantics=("parallel", "parallel"), vmem_limit_bytes=VMEM_LIMIT),
        name="post",
    )(*operands)


def _layer(x, w, tabs, *, final_norm, g_final):
    tab_a, tab_bq, tab_bk = tabs
    qa, ka, va, b0, b1, b2, gates = _inproj(
        x, w["w_in"], w["g_mix"], w["q_gain"], w["k_gain"], tab_a, tab_bq, tab_bk, tn=512)
    ya = _attn_a(qa, ka, va, tq=256, tk=512)
    b_outs = [_attn_b(b) for b in (b0, b1, b2)]
    return _post(x, ya, b_outs, gates, w["w_a_out"], w["w_b_out"], w["w_out"], w["g_mlp"],
                 w["w_up"], w["w_down"], g_final, tn=512, final_norm=final_norm)


def _trunk(x, layers, g_final):
    seq = x.shape[1]
    tab_a, tab_b = _rope_tables(seq)
    tabs = (tab_a, tab_b * Q_SCALE, tab_b)
    for i, w in enumerate(layers):
        x = _layer(x, w, tabs, final_norm=(i == len(layers) - 1), g_final=g_final)
    return x


def kernel(x_prompt, x_sample, w_in, w_a_out, w_b_out, w_out, g_mix, q_gain, k_gain, g_mlp,
           w_up, w_down, g_final):
    depth = w_in.shape[0]
    reps = LANES // HEAD_DIM
    layers = []
    for l in range(depth):
        layers.append(dict(
            w_in=w_in[l].astype(BF16), w_a_out=w_a_out[l].astype(BF16),
            w_b_out=w_b_out[l].astype(BF16), w_out=w_out[l].astype(BF16),
            w_up=w_up[l].astype(BF16), w_down=w_down[l].astype(BF16),
            g_mix=g_mix[l][None, :], g_mlp=g_mlp[l][None, :],
            q_gain=jnp.tile(q_gain[l] * Q_SCALE, reps)[None, :],
            k_gain=jnp.tile(k_gain[l], reps)[None, :],
        ))
    gf = g_final[None, :]
    return (_trunk(x_prompt, layers, gf), _trunk(x_sample, layers, gf))
```

```python
import functools
import math

import jax
import jax.numpy as jnp
from jax import lax
from jax.experimental import pallas as pl
from jax.experimental.pallas import tpu as pltpu

F32 = jnp.float32
BF16 = jnp.bfloat16

HEAD_DIM = 64
A_Q_HEADS = 8
A_KV_HEADS = 2
B_DILATIONS = (1, 4, 16)
B_HALF_WINDOW = 64
B_HEADS_PER_GROUP = 4
GRID_W = 64
AXIAL_THETA = 10000.0
PARTIAL_THETA = 500000.0
PARTIAL_ROPE_DIM = HEAD_DIM // 4
EPS = 1e-6
MASK_VALUE = -1e30

LANES = 128
A_Q_W = A_Q_HEADS * HEAD_DIM
A_KV_W = A_KV_HEADS * HEAD_DIM
B_GROUP_W = B_HEADS_PER_GROUP * HEAD_DIM
B_W = len(B_DILATIONS) * B_GROUP_W
Q_SCALE = HEAD_DIM ** -0.5 * math.log2(math.e)

VMEM_LIMIT = 56 * 1024 * 1024


def _const_spec(shape):
    return pl.BlockSpec(shape, lambda *_: (0,) * len(shape), pipeline_mode=pl.Buffered(1))


def _rope_tables(seq):
    t = jnp.arange(seq)

    def half_tables(pos, d, theta):
        d2 = d // 2
        freqs = theta ** (-(jnp.arange(d2, dtype=F32) * 2.0) / d)
        ang = pos.astype(F32)[:, None] * freqs[None, :]
        cos, sin = jnp.cos(ang), jnp.sin(ang)
        zero = jnp.zeros_like(sin)
        return (jnp.concatenate([cos, cos], -1), jnp.concatenate([-sin, zero], -1),
                jnp.concatenate([zero, sin], -1))

    half = HEAD_DIM // 2
    row = half_tables(t // GRID_W, half, AXIAL_THETA)
    col = half_tables(t % GRID_W, half, AXIAL_THETA)
    axial = [jnp.concatenate([r, c], -1) for r, c in zip(row, col)]
    part = half_tables(t, PARTIAL_ROPE_DIM, PARTIAL_THETA)
    rest = HEAD_DIM - PARTIAL_ROPE_DIM
    fill = (jnp.ones((seq, rest), F32), jnp.zeros((seq, rest), F32), jnp.zeros((seq, rest), F32))
    partial = [jnp.concatenate([p, f], -1) for p, f in zip(part, fill)]
    reps = LANES // HEAD_DIM
    tab_a = jnp.stack([jnp.tile(a, (1, reps)) for a in axial])
    tab_b = jnp.stack([jnp.tile(p, (1, reps)) for p in partial])
    return tab_a, tab_b


def _inproj_kernel(x_ref, gmix_ref, w_ref, qg_ref, kg_ref, ta_ref, tbq_ref, tbk_ref,
                   qa_ref, ka_ref, va_ref, b0_ref, b1_ref, b2_ref, gate_ref, dl_ref, *, tn):
    x = x_ref[0]
    ms = jnp.mean(x * x, axis=-1, keepdims=True)
    h = (x * lax.rsqrt(ms + EPS) * gmix_ref[...]).astype(BF16)

    lane = lax.broadcasted_iota(jnp.int32, (tn, LANES), 1)
    low = lane < HEAD_DIM
    rr = lax.broadcasted_iota(jnp.int32, (LANES, LANES), 0) // HEAD_DIM
    cc = lax.broadcasted_iota(jnp.int32, (LANES, LANES), 1) // HEAD_DIM
    seg_ones = jnp.where(rr == cc, 1.0, 0.0).astype(BF16)

    def proj(c0, c1):
        return jnp.dot(h, w_ref[:, c0:c1], preferred_element_type=F32)

    def head_norm(z, gain):
        z2 = z * z
        hi = z2.astype(BF16)
        lo = (z2 - hi.astype(F32)).astype(BF16)
        ss = (jnp.dot(hi, seg_ones, preferred_element_type=F32)
              + jnp.dot(lo, seg_ones, preferred_element_type=F32))
        return z * lax.rsqrt(ss * (1.0 / HEAD_DIM) + EPS) * gain

    def rope(z, tab_ref, sh):
        return (z * tab_ref[0] + pltpu.roll(z, LANES - sh, 1) * tab_ref[1]
                + pltpu.roll(z, sh, 1) * tab_ref[2])

    def pair_rep(z):
        zr = pltpu.roll(z, HEAD_DIM, 1)
        return jnp.where(low, z, zr), jnp.where(low, zr, z)

    a_sh = HEAD_DIM // 4
    for blk in range(A_Q_W // LANES):
        z = proj(blk * LANES, (blk + 1) * LANES)
        z = rope(head_norm(z, qg_ref[...]), ta_ref, a_sh)
        qa_ref[0, :, blk * LANES:(blk + 1) * LANES] = z.astype(BF16)
    zk = rope(head_norm(proj(A_Q_W, A_Q_W + A_KV_W), kg_ref[...]), ta_ref, a_sh)
    k0, k1 = pair_rep(zk)
    ka_ref[0, 0] = k0.astype(BF16)
    ka_ref[0, 1] = k1.astype(BF16)
    zv = proj(A_Q_W + A_KV_W, A_Q_W + 2 * A_KV_W)
    va_ref[0, 0] = jnp.where(low, zv, 1.0).astype(BF16)
    va_ref[0, 1] = jnp.where(low, pltpu.roll(zv, HEAD_DIM, 1), 1.0).astype(BF16)

    b_base = A_Q_W + 2 * A_KV_W
    b_refs = (b0_ref, b1_ref, b2_ref)
    b_sh = PARTIAL_ROPE_DIM // 2
    for t, tab_ref in enumerate((tbq_ref, tbk_ref, None)):
        for g, d in enumerate(B_DILATIONS):
            c0 = b_base + t * B_W + g * B_GROUP_W
            z = proj(c0, c0 + B_GROUP_W)
            halves = []
            for s in range(B_GROUP_W // LANES):
                zs = z[:, s * LANES:(s + 1) * LANES]
                halves.append(zs if tab_ref is None else rope(zs, tab_ref, b_sh))
            col = t * B_GROUP_W
            if d == 1:
                for s, zs in enumerate(halves):
                    b_refs[g][0, 0, :, col + s * LANES:col + (s + 1) * LANES] = zs.astype(BF16)
            else:
                for s, zs in enumerate(halves):
                    dl_ref[s] = zs
                for r in range(d):
                    for s in range(len(halves)):
                        piece = dl_ref[s, pl.ds(r, tn // d, stride=d), :]
                        b_refs[g][0, r, :, col + s * LANES:col + (s + 1) * LANES] = piece.astype(BF16)

    g_base = b_base + 3 * B_W
    n_gate = gate_ref.shape[-1]
    for c in range(n_gate // 256):
        z = proj(g_base + c * 256, g_base + (c + 1) * 256)
        gate_ref[0, :, c * 256:(c + 1) * 256] = z.astype(BF16)


def _inproj(x, w_in, g_mix, q_gain, k_gain, tab_a, tab_bq, tab_bk, *, tn):
    bn, seq, dm = x.shape
    in_w = w_in.shape[1]
    gate_w = in_w - (A_Q_W + 2 * A_KV_W + 3 * B_W)
    grid = (bn, seq // tn)
    tab_spec = pl.BlockSpec((3, tn, LANES), lambda b, i: (0, i, 0))
    out_shape = (
        jax.ShapeDtypeStruct((bn, seq, A_Q_W), BF16),
        jax.ShapeDtypeStruct((bn, A_KV_HEADS, seq, LANES), BF16),
        jax.ShapeDtypeStruct((bn, A_KV_HEADS, seq, LANES), BF16),
    ) + tuple(jax.ShapeDtypeStruct((bn, d, seq // d, 3 * B_GROUP_W), BF16) for d in B_DILATIONS) + (
        jax.ShapeDtypeStruct((bn, seq, gate_w), BF16),
    )
    out_specs = (
        pl.BlockSpec((1, tn, A_Q_W), lambda b, i: (b, i, 0)),
        pl.BlockSpec((1, A_KV_HEADS, tn, LANES), lambda b, i: (b, 0, i, 0)),
        pl.BlockSpec((1, A_KV_HEADS, tn, LANES), lambda b, i: (b, 0, i, 0)),
    ) + tuple(pl.BlockSpec((1, d, tn // d, 3 * B_GROUP_W), lambda b, i: (b, 0, i, 0))
              for d in B_DILATIONS) + (
        pl.BlockSpec((1, tn, gate_w), lambda b, i: (b, i, 0)),
    )
    return pl.pallas_call(
        functools.partial(_inproj_kernel, tn=tn),
        grid=grid,
        in_specs=[
            pl.BlockSpec((1, tn, dm), lambda b, i: (b, i, 0)),
            _const_spec((1, dm)),
            _const_spec((dm, in_w)),
            _const_spec((1, LANES)),
            _const_spec((1, LANES)),
            tab_spec, tab_spec, tab_spec,
        ],
        out_specs=out_specs,
        out_shape=out_shape,
        scratch_shapes=[pltpu.VMEM((B_GROUP_W // LANES, tn, LANES), F32)],
        compiler_params=pltpu.CompilerParams(
            dimension_semantics=("parallel", "parallel"), vmem_limit_bytes=VMEM_LIMIT),
        name="inproj",
    )(x, g_mix, w_in, q_gain, k_gain, tab_a, tab_bq, tab_bk)


def _attn_a_kernel(q_ref, k_ref, v_ref, o_ref, qs_ref, s_ref, pm_ref, m_ref, acc_ref,
                   *, tq, tk, nk):
    group = A_Q_HEADS // A_KV_HEADS
    n_lc = tk // LANES
    lane = lax.broadcasted_iota(jnp.int32, (tq, LANES), 1)
    low = lane < HEAD_DIM
    for h in range(group):
        pair = q_ref[0, :, (h // 2) * LANES:(h // 2 + 1) * LANES].astype(F32)
        keep = low if h % 2 == 0 else jnp.logical_not(low)
        qs_ref[h * tq:(h + 1) * tq, :] = jnp.where(keep, pair, 0.0).astype(BF16)
    m_ref[...] = jnp.full(m_ref.shape, MASK_VALUE, F32)
    acc_ref[...] = jnp.zeros(acc_ref.shape, F32)

    def scores(c):
        k = k_ref[0, 0, c * tk:(c + 1) * tk, :]
        s = lax.dot_general(qs_ref[...], k, (((1,), (1,)), ((), ())),
                            preferred_element_type=F32)
        s_ref[c % 2] = s
        pm = s[:, :LANES]
        for j in range(1, n_lc):
            pm = jnp.maximum(pm, s[:, j * LANES:(j + 1) * LANES])
        pm_ref[c % 2] = pm

    scores(0)
    for c in range(nk):
        if c + 1 < nk:
            scores(c + 1)
        slot = c % 2
        m_prev = m_ref[...]
        m_next = jnp.maximum(m_prev, jnp.max(pm_ref[slot], axis=1, keepdims=True))
        alpha = jnp.exp2(m_prev - m_next)
        m_ref[...] = m_next
        p = jnp.concatenate(
            [jnp.exp2(s_ref[slot, :, j * LANES:(j + 1) * LANES] - m_next) for j in range(n_lc)],
            axis=1).astype(BF16)
        v = v_ref[0, 0, c * tk:(c + 1) * tk, :]
        acc_ref[...] = alpha * acc_ref[...] + jnp.dot(p, v, preferred_element_type=F32)

    acc = acc_ref[...]
    acc_sw = pltpu.roll(acc, HEAD_DIM, 1)
    for pr in range(group // 2):
        ev = slice((2 * pr) * tq, (2 * pr + 1) * tq)
        od = slice((2 * pr + 1) * tq, (2 * pr + 2) * tq)
        out = jnp.where(low, acc[ev] / acc_sw[ev], acc_sw[od] / acc[od])
        o_ref[0, :, pr * LANES:(pr + 1) * LANES] = out.astype(BF16)


def _attn_a(qa, ka, va, *, tq, tk):
    bn, seq, _ = qa.shape
    group = A_Q_HEADS // A_KV_HEADS
    gw = group * HEAD_DIM
    grid = (bn, A_KV_HEADS, seq // tq)
    kv_spec = pl.BlockSpec((1, 1, seq, LANES), lambda b, j, i: (b, j, 0, 0))
    return pl.pallas_call(
        functools.partial(_attn_a_kernel, tq=tq, tk=tk, nk=seq // tk),
        grid=grid,
        in_specs=[pl.BlockSpec((1, tq, gw), lambda b, j, i: (b, i, j)), kv_spec, kv_spec],
        out_specs=pl.BlockSpec((1, tq, gw), lambda b, j, i: (b, i, j)),
        out_shape=jax.ShapeDtypeStruct((bn, seq, A_Q_W), BF16),
        scratch_shapes=[
            pltpu.VMEM((group * tq, LANES), BF16),
            pltpu.VMEM((2, group * tq, tk), F32),
            pltpu.VMEM((2, group * tq, LANES), F32),
            pltpu.VMEM((group * tq, LANES), F32),
            pltpu.VMEM((group * tq, LANES), F32),
        ],
        compiler_params=pltpu.CompilerParams(
            dimension_semantics=("parallel", "parallel", "parallel"), vmem_limit_bytes=VMEM_LIMIT),
        name="attn_a",
    )(qa, ka, va)


def _attn_b_kernel(q_ref, k_ref, v_ref, hmask_ref, bias_ref, o_ref, lse_ref,
                   *, sub_len, tq, tk, rb, n_il):
    nh = B_HEADS_PER_GROUP
    blk = lax.broadcasted_iota(jnp.int32, (tq, B_GROUP_W), 1) // HEAD_DIM

    def tile(r, i0, w0):
        q = q_ref[0, r, pl.ds(i0, tq), :]
        k = k_ref[0, r, pl.ds(w0, tk), :]
        v = v_ref[0, r, pl.ds(w0, tk), :]
        qs = jnp.concatenate([q * hmask_ref[h] for h in range(nh)], axis=0)
        s = lax.dot_general(qs, k, (((1,), (1,)), ((), ())), preferred_element_type=F32)
        bias = bias_ref[(i0 - w0) // B_HALF_WINDOW]
        s = s + jnp.concatenate([bias] * nh, axis=0)
        m = jnp.max(s, axis=1, keepdims=True)
        p = jnp.exp2(s - m)
        l = jnp.sum(p, axis=1, keepdims=True)
        pv = jnp.dot(p.astype(BF16), v, preferred_element_type=F32)
        lse = m + jnp.log2(l)
        num = pv[:tq]
        den = jnp.broadcast_to(l[:tq], (tq, B_GROUP_W))
        lse_out = jnp.broadcast_to(lse[:tq], (tq, B_GROUP_W))
        for h in range(1, nh):
            sel = blk == h
            rows = slice(h * tq, (h + 1) * tq)
            num = jnp.where(sel, pv[rows], num)
            den = jnp.where(sel, l[rows], den)
            lse_out = jnp.where(sel, lse[rows], lse_out)
        o_ref[0, r, pl.ds(i0, tq), :] = (num / den).astype(BF16)
        lse_ref[0, r, pl.ds(i0, tq), :] = lse_out

    per_res = sub_len // tq
    if per_res <= n_il:
        for r in range(rb):
            for t in range(per_res):
                i0 = t * tq
                tile(r, i0, max(0, min(i0 - B_HALF_WINDOW, sub_len - tk)))
    else:
        def group_body(gi, carry):
            for j in range(n_il):
                i0 = pl.multiple_of((gi * n_il + j) * tq, tq)
                w0 = jnp.clip(i0 - B_HALF_WINDOW, 0, sub_len - tk)
                tile(0, i0, pl.multiple_of(w0, B_HALF_WINDOW))
            return carry

        lax.fori_loop(0, per_res // n_il, group_body, 0)


def _attn_b(qkv, *, tq=128, n_il=8):
    bn, d, sub_len, _ = qkv.shape
    tq = min(tq, sub_len)
    tk = min(tq + 2 * B_HALF_WINDOW, sub_len)
    per_res = sub_len // tq
    rb = min(d, max(1, n_il // per_res))
    lane_head = jnp.arange(B_GROUP_W) // HEAD_DIM
    hmask = (lane_head[None, None, :] == jnp.arange(B_HEADS_PER_GROUP)[:, None, None])
    hmask = jnp.broadcast_to(hmask, (B_HEADS_PER_GROUP, tq, B_GROUP_W)).astype(BF16)
    rel = jnp.arange(tq)[:, None] - jnp.arange(tk)[None, :]
    offs = jnp.arange(3)[:, None, None] * B_HALF_WINDOW
    bias = jnp.where(jnp.abs(rel[None] + offs) <= B_HALF_WINDOW, 0.0, MASK_VALUE).astype(F32)

    def spec(c):
        return pl.BlockSpec((1, rb, sub_len, B_GROUP_W), lambda b, r: (b, r, 0, c))

    out_spec = pl.BlockSpec((1, rb, sub_len, B_GROUP_W), lambda b, r: (b, r, 0, 0))
    return pl.pallas_call(
        functools.partial(_attn_b_kernel, sub_len=sub_len, tq=tq, tk=tk, rb=rb, n_il=n_il),
        grid=(bn, d // rb),
        in_specs=[spec(0), spec(1), spec(2), _const_spec(hmask.shape), _const_spec(bias.shape)],
        out_specs=(out_spec, out_spec),
        out_shape=(jax.ShapeDtypeStruct((bn, d, sub_len, B_GROUP_W), BF16),
                   jax.ShapeDtypeStruct((bn, d, sub_len, B_GROUP_W), F32)),
        compiler_params=pltpu.CompilerParams(
            dimension_semantics=("parallel", "parallel"), vmem_limit_bytes=VMEM_LIMIT),
        name="attn_b",
    )(qkv, qkv, qkv, hmask, bias)


def _post_kernel(x_ref, ya_ref, o0_ref, l0_ref, o1_ref, l1_ref, o2_ref, l2_ref, gate_ref,
                 wa_ref, wb_ref, wo_ref, gmlp_ref, wup_ref, wdn_ref, gfin_ref,
                 y_ref, ril_ref, *, tn, ff_chunk, final_norm):
    dm = x_ref.shape[-1]
    n_slab = B_GROUP_W // LANES

    def natural_order(src_ref, d, slot):
        for r in range(d):
            blk = src_ref[0, r].astype(F32)
            for s in range(n_slab):
                ril_ref[slot * n_slab + s, pl.ds(r, tn // d, stride=d), :] = blk[:, s * LANES:(s + 1) * LANES]
        return jnp.concatenate([ril_ref[slot * n_slab + s] for s in range(n_slab)], axis=1)

    o0 = o0_ref[0, 0].astype(F32)
    l0 = l0_ref[0, 0]
    o1 = natural_order(o1_ref, B_DILATIONS[1], 0)
    l1 = natural_order(l1_ref, B_DILATIONS[1], 1)
    o2 = natural_order(o2_ref, B_DILATIONS[2], 2)
    l2 = natural_order(l2_ref, B_DILATIONS[2], 3)
    mx = jnp.maximum(jnp.maximum(l0, l1), l2)
    e0, e1, e2 = jnp.exp2(l0 - mx), jnp.exp2(l1 - mx), jnp.exp2(l2 - mx)
    yb_in = (e0 * o0 + e1 * o1 + e2 * o2) / (e0 + e1 + e2)

    ya = jnp.dot(ya_ref[0], wa_ref[...], preferred_element_type=F32)
    yb = jnp.dot(yb_in.astype(BF16), wb_ref[...], preferred_element_type=F32)
    ga = gate_ref[0, :, :dm].astype(F32)
    gb = gate_ref[0, :, dm:].astype(F32)
    mixed = ya / (1.0 + jnp.exp(-ga)) + yb / (1.0 + jnp.exp(-gb))
    x1 = x_ref[0] + jnp.dot(mixed.astype(BF16), wo_ref[...], preferred_element_type=F32)

    ms = jnp.mean(x1 * x1, axis=-1, keepdims=True)
    h2 = (x1 * lax.rsqrt(ms + EPS) * gmlp_ref[...]).astype(BF16)
    acc = x1
    d_ff = wup_ref.shape[1]
    for c in range(d_ff // ff_chunk):
        u = jnp.dot(h2, wup_ref[:, c * ff_chunk:(c + 1) * ff_chunk], preferred_element_type=F32)
        u = jnp.square(jnp.maximum(u, 0.0)).astype(BF16)
        acc = acc + jnp.dot(u, wdn_ref[c * ff_chunk:(c + 1) * ff_chunk, :], preferred_element_type=F32)
    if final_norm:
        ms = jnp.mean(acc * acc, axis=-1, keepdims=True)
        acc = acc * lax.rsqrt(ms + EPS) * gfin_ref[...]
    y_ref[0] = acc


def _post(x, ya, b_outs, gates, w_a_out, w_b_out, w_out, g_mlp, w_up, w_down, g_final,
          *, tn, final_norm, ff_chunk=512):
    bn, seq, dm = x.shape
    d_ff = w_up.shape[1]
    grid = (bn, seq // tn)
    in_specs = [
        pl.BlockSpec((1, tn, dm), lambda b, i: (b, i, 0)),
        pl.BlockSpec((1, tn, A_Q_W), lambda b, i: (b, i, 0)),
    ]
    operands = [x, ya]
    for d, (o, lse) in zip(B_DILATIONS, b_outs):
        spec = pl.BlockSpec((1, d, tn // d, B_GROUP_W), lambda b, i: (b, 0, i, 0))
        in_specs += [spec, spec]
        operands += [o, lse]
    in_specs += [
        pl.BlockSpec((1, tn, gates.shape[-1]), lambda b, i: (b, i, 0)),
        _const_spec(w_a_out.shape), _const_spec(w_b_out.shape), _const_spec(w_out.shape),
        _const_spec((1, dm)), _const_spec(w_up.shape), _const_spec(w_down.shape), _const_spec((1, dm)),
    ]
    operands += [gates, w_a_out, w_b_out, w_out, g_mlp, w_up, w_down, g_final]
    n_ril = 4 * (B_GROUP_W // LANES)
    return pl.pallas_call(
        functools.partial(_post_kernel, tn=tn, ff_chunk=ff_chunk, final_norm=final_norm),
        grid=grid,
        in_specs=in_specs,
        out_specs=pl.BlockSpec((1, tn, dm), lambda b, i: (b, i, 0)),
        out_shape=jax.ShapeDtypeStruct((bn, seq, dm), F32),
        scratch_shapes=[pltpu.VMEM((n_ril, tn, LANES), F32)],
        compiler_params=pltpu.CompilerParams(
            dimension_semantics=("parallel", "parallel"), vmem_limit_bytes=VMEM_LIMIT),
        name="post",
    )(*operands)


def _layer(x, w, tabs, *, final_norm, g_final):
    tab_a, tab_bq, tab_bk = tabs
    qa, ka, va, b0, b1, b2, gates = _inproj(
        x, w["w_in"], w["g_mix"], w["q_gain"], w["k_gain"], tab_a, tab_bq, tab_bk, tn=512)
    ya = _attn_a(qa, ka, va, tq=256, tk=512)
    b_outs = [_attn_b(b) for b in (b0, b1, b2)]
    return _post(x, ya, b_outs, gates, w["w_a_out"], w["w_b_out"], w["w_out"], w["g_mlp"],
                 w["w_up"], w["w_down"], g_final, tn=512, final_norm=final_norm)


def _trunk(x, layers, g_final):
    seq = x.shape[1]
    tab_a, tab_b = _rope_tables(seq)
    tabs = (tab_a, tab_b * Q_SCALE, tab_b)
    for i, w in enumerate(layers):
        x = _layer(x, w, tabs, final_norm=(i == len(layers) - 1), g_final=g_final)
    return x


def kernel(x_prompt, x_sample, w_in, w_a_out, w_b_out, w_out, g_mix, q_gain, k_gain, g_mlp,
           w_up, w_down, g_final):
    depth = w_in.shape[0]
    reps = LANES // HEAD_DIM
    layers = []
    for l in range(depth):
        layers.append(dict(
            w_in=w_in[l].astype(BF16), w_a_out=w_a_out[l].astype(BF16),
            w_b_out=w_b_out[l].astype(BF16), w_out=w_out[l].astype(BF16),
            w_up=w_up[l].astype(BF16), w_down=w_down[l].astype(BF16),
            g_mix=g_mix[l][None, :], g_mlp=g_mlp[l][None, :],
            q_gain=jnp.tile(q_gain[l] * Q_SCALE, reps)[None, :],
            k_gain=jnp.tile(k_gain[l], reps)[None, :],
        ))
    gf = g_final[None, :]
    return (_trunk(x_prompt, layers, gf), _trunk(x_sample, layers, gf))
```

```python
import functools
import math

import jax
import jax.numpy as jnp
from jax import lax
from jax.experimental import pallas as pl
from jax.experimental.pallas import tpu as pltpu

F32 = jnp.float32
BF16 = jnp.bfloat16

HEAD_DIM = 64
A_Q_HEADS = 8
A_KV_HEADS = 2
B_DILATIONS = (1, 4, 16)
B_HALF_WINDOW = 64
B_HEADS_PER_GROUP = 4
GRID_W = 64
AXIAL_THETA = 10000.0
PARTIAL_THETA = 500000.0
PARTIAL_ROPE_DIM = HEAD_DIM // 4
EPS = 1e-6
MASK_VALUE = -1e30

LANES = 128
A_Q_W = A_Q_HEADS * HEAD_DIM
A_KV_W = A_KV_HEADS * HEAD_DIM
B_GROUP_W = B_HEADS_PER_GROUP * HEAD_DIM
B_W = len(B_DILATIONS) * B_GROUP_W
Q_SCALE = HEAD_DIM ** -0.5 * math.log2(math.e)

PROJ_CHUNK = 512
VMEM_LIMIT = 56 * 1024 * 1024


def _const_spec(shape):
    return pl.BlockSpec(shape, lambda *_: (0,) * len(shape), pipeline_mode=pl.Buffered(1))


def _rope_tables(seq):
    t = jnp.arange(seq)

    def half_tables(pos, d, theta):
        d2 = d // 2
        freqs = theta ** (-(jnp.arange(d2, dtype=F32) * 2.0) / d)
        ang = pos.astype(F32)[:, None] * freqs[None, :]
        cos, sin = jnp.cos(ang), jnp.sin(ang)
        zero = jnp.zeros_like(sin)
        return (jnp.concatenate([cos, cos], -1), jnp.concatenate([-sin, zero], -1),
                jnp.concatenate([zero, sin], -1))

    half = HEAD_DIM // 2
    row = half_tables(t // GRID_W, half, AXIAL_THETA)
    col = half_tables(t % GRID_W, half, AXIAL_THETA)
    axial = [jnp.concatenate([r, c], -1) for r, c in zip(row, col)]
    part = half_tables(t, PARTIAL_ROPE_DIM, PARTIAL_THETA)
    rest = HEAD_DIM - PARTIAL_ROPE_DIM
    fill = (jnp.ones((seq, rest), F32), jnp.zeros((seq, rest), F32), jnp.zeros((seq, rest), F32))
    partial = [jnp.concatenate([p, f], -1) for p, f in zip(part, fill)]
    reps = LANES // HEAD_DIM
    tab_a = jnp.stack([jnp.tile(a, (1, reps)) for a in axial])
    tab_b = jnp.stack([jnp.tile(p, (1, reps)) for p in partial])
    return tab_a, tab_b


def _inproj_kernel(x_ref, gmix_ref, w_ref, qg_ref, kg_ref, seg_ref, ta_ref, tbq_ref, tbk_ref,
                   qa_ref, ka_ref, va_ref, b0_ref, b1_ref, b2_ref, gate_ref, z_ref, dl_ref, *, tn):
    x = x_ref[0]
    ms = jnp.mean(x * x, axis=-1, keepdims=True)
    h = (x * lax.rsqrt(ms + EPS) * gmix_ref[...]).astype(BF16)
    low = lax.broadcasted_iota(jnp.int32, (tn, LANES), 1) < HEAD_DIM
    n_slab = PROJ_CHUNK // LANES
    n_chunk = w_ref.shape[1] // PROJ_CHUNK

    def stage(c):
        z = jnp.dot(h, w_ref[:, c * PROJ_CHUNK:(c + 1) * PROJ_CHUNK], preferred_element_type=F32)
        for s in range(n_slab):
            z_ref[c % 2, s] = z[:, s * LANES:(s + 1) * LANES]

    def slab(col):
        c, rem = divmod(col, PROJ_CHUNK)
        return z_ref.at[c % 2, rem // LANES]

    def inv_rms(zs, seg):
        z2 = jnp.concatenate([z * z for z in zs], axis=1).astype(BF16)
        ss = jnp.dot(z2, seg, preferred_element_type=F32)
        return lax.rsqrt(ss * (1.0 / HEAD_DIM) + EPS)

    def rope(z, tab_ref, sh):
        return (z * tab_ref[0] + pltpu.roll(z, LANES - sh, 1) * tab_ref[1]
                + pltpu.roll(z, sh, 1) * tab_ref[2])

    a_sh = HEAD_DIM // 4
    b_sh = PARTIAL_ROPE_DIM // 2
    kv_col = A_Q_W
    b_col = A_Q_W + 2 * A_KV_W
    g_col = b_col + 3 * B_W
    b_refs = (b0_ref, b1_ref, b2_ref)
    dl_slot = [0]

    def epilogue_a_q(col):
        zs = [slab(col)[...], slab(col + LANES)[...]]
        inv = inv_rms(zs, seg_ref[...])
        for i, z in enumerate(zs):
            zn = z * inv[:, i * LANES:(i + 1) * LANES] * qg_ref[...]
            c0 = col + i * LANES
            qa_ref[0, :, c0:c0 + LANES] = rope(zn, ta_ref, a_sh).astype(BF16)

    def epilogue_a_kv(col):
        zk = slab(col)[...]
        zk = rope(zk * inv_rms([zk], seg_ref[:LANES, :LANES]) * kg_ref[...], ta_ref, a_sh)
        zkr = pltpu.roll(zk, HEAD_DIM, 1)
        ka_ref[0, 0] = jnp.where(low, zk, zkr).astype(BF16)
        ka_ref[0, 1] = jnp.where(low, zkr, zk).astype(BF16)
        zv = slab(col + LANES)[...]
        va_ref[0, 0] = jnp.where(low, zv, 1.0).astype(BF16)
        va_ref[0, 1] = jnp.where(low, pltpu.roll(zv, HEAD_DIM, 1), 1.0).astype(BF16)

    def epilogue_b(col):
        t, rem = divmod(col - b_col, B_W)
        g, rem = divmod(rem, B_GROUP_W)
        d = B_DILATIONS[g]
        out_col = t * B_GROUP_W + rem
        tab_ref = (tbq_ref, tbk_ref, None)[t]
        src = slab(col)
        if tab_ref is not None:
            z = rope(src[...], tab_ref, b_sh)
            if d == 1:
                b_refs[g][0, 0, :, out_col:out_col + LANES] = z.astype(BF16)
                return
            src = dl_ref.at[dl_slot[0]]
            dl_slot[0] = 1 - dl_slot[0]
            src[...] = z
        elif d == 1:
            b_refs[g][0, 0, :, out_col:out_col + LANES] = src[...].astype(BF16)
            return
        for r in range(d):
            piece = src[pl.ds(r, tn // d, stride=d), :]
            b_refs[g][0, r, :, out_col:out_col + LANES] = piece.astype(BF16)

    def epilogue(c):
        col = c * PROJ_CHUNK
        while col < (c + 1) * PROJ_CHUNK:
            if col < kv_col:
                epilogue_a_q(col)
                col += 2 * LANES
            elif col < b_col:
                epilogue_a_kv(col)
                col += 2 * LANES
            elif col < g_col:
                epilogue_b(col)
                col += LANES
            else:
                gate_ref[0, :, col - g_col:col - g_col + LANES] = slab(col)[...].astype(BF16)
                col += LANES

    stage(0)
    for c in range(n_chunk):
        if c + 1 < n_chunk:
            stage(c + 1)
        epilogue(c)


def _inproj(x, w_in, g_mix, q_gain, k_gain, tab_a, tab_bq, tab_bk, *, tn):
    bn, seq, dm = x.shape
    in_w = w_in.shape[1]
    gate_w = in_w - (A_Q_W + 2 * A_KV_W + 3 * B_W)
    grid = (bn, seq // tn)
    tab_spec = pl.BlockSpec((3, tn, LANES), lambda b, i: (0, i, 0))
    head_of = jnp.arange(2 * LANES) // HEAD_DIM
    seg = (head_of[:, None] == head_of[None, :]).astype(BF16)
    out_shape = (
        jax.ShapeDtypeStruct((bn, seq, A_Q_W), BF16),
        jax.ShapeDtypeStruct((bn, A_KV_HEADS, seq, LANES), BF16),
        jax.ShapeDtypeStruct((bn, A_KV_HEADS, seq, LANES), BF16),
    ) + tuple(jax.ShapeDtypeStruct((bn, d, seq // d, 3 * B_GROUP_W), BF16) for d in B_DILATIONS) + (
        jax.ShapeDtypeStruct((bn, seq, gate_w), BF16),
    )
    out_specs = (
        pl.BlockSpec((1, tn, A_Q_W), lambda b, i: (b, i, 0)),
        pl.BlockSpec((1, A_KV_HEADS, tn, LANES), lambda b, i: (b, 0, i, 0)),
        pl.BlockSpec((1, A_KV_HEADS, tn, LANES), lambda b, i: (b, 0, i, 0)),
    ) + tuple(pl.BlockSpec((1, d, tn // d, 3 * B_GROUP_W), lambda b, i: (b, 0, i, 0))
              for d in B_DILATIONS) + (
        pl.BlockSpec((1, tn, gate_w), lambda b, i: (b, i, 0)),
    )
    return pl.pallas_call(
        functools.partial(_inproj_kernel, tn=tn),
        grid=grid,
        in_specs=[
            pl.BlockSpec((1, tn, dm), lambda b, i: (b, i, 0)),
            _const_spec((1, dm)),
            _const_spec((dm, in_w)),
            _const_spec((1, LANES)),
            _const_spec((1, LANES)),
            _const_spec(seg.shape),
            tab_spec, tab_spec, tab_spec,
        ],
        out_specs=out_specs,
        out_shape=out_shape,
        scratch_shapes=[pltpu.VMEM((2, PROJ_CHUNK // LANES, tn, LANES), F32),
                        pltpu.VMEM((2, tn, LANES), F32)],
        compiler_params=pltpu.CompilerParams(
            dimension_semantics=("parallel", "parallel"), vmem_limit_bytes=VMEM_LIMIT),
        name="inproj",
    )(x, g_mix, w_in, q_gain, k_gain, seg, tab_a, tab_bq, tab_bk)


def _attn_a_kernel(q_ref, k_ref, v_ref, o_ref, qs_ref, s_ref, pm_ref, m_ref, acc_ref,
                   *, tq, tk, nk):
    group = A_Q_HEADS // A_KV_HEADS
    n_lc = tk // LANES
    lane = lax.broadcasted_iota(jnp.int32, (tq, LANES), 1)
    low = lane < HEAD_DIM
    for h in range(group):
        pair = q_ref[0, :, (h // 2) * LANES:(h // 2 + 1) * LANES].astype(F32)
        keep = low if h % 2 == 0 else jnp.logical_not(low)
        qs_ref[h * tq:(h + 1) * tq, :] = jnp.where(keep, pair, 0.0).astype(BF16)
    m_ref[...] = jnp.full(m_ref.shape, MASK_VALUE, F32)
    acc_ref[...] = jnp.zeros(acc_ref.shape, F32)

    def scores(c):
        k = k_ref[0, 0, c * tk:(c + 1) * tk, :]
        s = lax.dot_general(qs_ref[...], k, (((1,), (1,)), ((), ())),
                            preferred_element_type=F32)
        s_ref[c % 2] = s
        pm = s[:, :LANES]
        for j in range(1, n_lc):
            pm = jnp.maximum(pm, s[:, j * LANES:(j + 1) * LANES])
        pm_ref[c % 2] = pm

    scores(0)
    for c in range(nk):
        if c + 1 < nk:
            scores(c + 1)
        slot = c % 2
        m_prev = m_ref[...]
        m_next = jnp.maximum(m_prev, jnp.max(pm_ref[slot], axis=1, keepdims=True))
        alpha = jnp.exp2(m_prev - m_next)
        m_ref[...] = m_next
        p = jnp.concatenate(
            [jnp.exp2(s_ref[slot, :, j * LANES:(j + 1) * LANES] - m_next) for j in range(n_lc)],
            axis=1).astype(BF16)
        v = v_ref[0, 0, c * tk:(c + 1) * tk, :]
        acc_ref[...] = alpha * acc_ref[...] + jnp.dot(p, v, preferred_element_type=F32)

    acc = acc_ref[...]
    acc_sw = pltpu.roll(acc, HEAD_DIM, 1)
    for pr in range(group // 2):
        ev = slice((2 * pr) * tq, (2 * pr + 1) * tq)
        od = slice((2 * pr + 1) * tq, (2 * pr + 2) * tq)
        out = jnp.where(low, acc[ev] / acc_sw[ev], acc_sw[od] / acc[od])
        o_ref[0, :, pr * LANES:(pr + 1) * LANES] = out.astype(BF16)


def _attn_a(qa, ka, va, *, tq, tk):
    bn, seq, _ = qa.shape
    group = A_Q_HEADS // A_KV_HEADS
    gw = group * HEAD_DIM
    grid = (bn, A_KV_HEADS, seq // tq)
    kv_spec = pl.BlockSpec((1, 1, seq, LANES), lambda b, j, i: (b, j, 0, 0))
    return pl.pallas_call(
        functools.partial(_attn_a_kernel, tq=tq, tk=tk, nk=seq // tk),
        grid=grid,
        in_specs=[pl.BlockSpec((1, tq, gw), lambda b, j, i: (b, i, j)), kv_spec, kv_spec],
        out_specs=pl.BlockSpec((1, tq, gw), lambda b, j, i: (b, i, j)),
        out_shape=jax.ShapeDtypeStruct((bn, seq, A_Q_W), BF16),
        scratch_shapes=[
            pltpu.VMEM((group * tq, LANES), BF16),
            pltpu.VMEM((2, group * tq, tk), F32),
            pltpu.VMEM((2, group * tq, LANES), F32),
            pltpu.VMEM((group * tq, LANES), F32),
            pltpu.VMEM((group * tq, LANES), F32),
        ],
        compiler_params=pltpu.CompilerParams(
            dimension_semantics=("parallel", "parallel", "parallel"), vmem_limit_bytes=VMEM_LIMIT),
        name="attn_a",
    )(qa, ka, va)


def _attn_b_kernel(q_ref, k_ref, v_ref, hmask_ref, bias_ref, o_ref, lse_ref,
                   *, sub_len, tq, tk, rb, n_il):
    nh = B_HEADS_PER_GROUP
    blk = lax.broadcasted_iota(jnp.int32, (tq, B_GROUP_W), 1) // HEAD_DIM

    def tile(r, i0, w0):
        q = q_ref[0, r, pl.ds(i0, tq), :]
        k = k_ref[0, r, pl.ds(w0, tk), :]
        v = v_ref[0, r, pl.ds(w0, tk), :]
        qs = jnp.concatenate([q * hmask_ref[h] for h in range(nh)], axis=0)
        s = lax.dot_general(qs, k, (((1,), (1,)), ((), ())), preferred_element_type=F32)
        bias = bias_ref[(i0 - w0) // B_HALF_WINDOW]
        s = s + jnp.concatenate([bias] * nh, axis=0)
        m = jnp.max(s, axis=1, keepdims=True)
        p = jnp.exp2(s - m)
        l = jnp.sum(p, axis=1, keepdims=True)
        pv = jnp.dot(p.astype(BF16), v, preferred_element_type=F32)
        lse = m + jnp.log2(l)
        num = pv[:tq]
        den = jnp.broadcast_to(l[:tq], (tq, B_GROUP_W))
        lse_out = jnp.broadcast_to(lse[:tq], (tq, B_GROUP_W))
        for h in range(1, nh):
            sel = blk == h
            rows = slice(h * tq, (h + 1) * tq)
            num = jnp.where(sel, pv[rows], num)
            den = jnp.where(sel, l[rows], den)
            lse_out = jnp.where(sel, lse[rows], lse_out)
        o_ref[0, r, pl.ds(i0, tq), :] = (num / den).astype(BF16)
        lse_ref[0, r, pl.ds(i0, tq), :] = lse_out

    per_res = sub_len // tq
    if per_res <= n_il:
        for r in range(rb):
            for t in range(per_res):
                i0 = t * tq
                tile(r, i0, max(0, min(i0 - B_HALF_WINDOW, sub_len - tk)))
    else:
        def group_body(gi, carry):
            for j in range(n_il):
                i0 = pl.multiple_of((gi * n_il + j) * tq, tq)
                w0 = jnp.clip(i0 - B_HALF_WINDOW, 0, sub_len - tk)
                tile(0, i0, pl.multiple_of(w0, B_HALF_WINDOW))
            return carry

        lax.fori_loop(0, per_res // n_il, group_body, 0)


def _attn_b(qkv, *, tq=128, n_il=8):
    bn, d, sub_len, _ = qkv.shape
    tq = min(tq, sub_len)
    tk = min(tq + 2 * B_HALF_WINDOW, sub_len)
    per_res = sub_len // tq
    rb = min(d, max(1, n_il // per_res))
    lane_head = jnp.arange(B_GROUP_W) // HEAD_DIM
    hmask = (lane_head[None, None, :] == jnp.arange(B_HEADS_PER_GROUP)[:, None, None])
    hmask = jnp.broadcast_to(hmask, (B_HEADS_PER_GROUP, tq, B_GROUP_W)).astype(BF16)
    rel = jnp.arange(tq)[:, None] - jnp.arange(tk)[None, :]
    offs = jnp.arange(3)[:, None, None] * B_HALF_WINDOW
    bias = jnp.where(jnp.abs(rel[None] + offs) <= B_HALF_WINDOW, 0.0, MASK_VALUE).astype(F32)

    def spec(c):
        return pl.BlockSpec((1, rb, sub_len, B_GROUP_W), lambda b, r: (b, r, 0, c))

    out_spec = pl.BlockSpec((1, rb, sub_len, B_GROUP_W), lambda b, r: (b, r, 0, 0))
    return pl.pallas_call(
        functools.partial(_attn_b_kernel, sub_len=sub_len, tq=tq, tk=tk, rb=rb, n_il=n_il),
        grid=(bn, d // rb),
        in_specs=[spec(0), spec(1), spec(2), _const_spec(hmask.shape), _const_spec(bias.shape)],
        out_specs=(out_spec, out_spec),
        out_shape=(jax.ShapeDtypeStruct((bn, d, sub_len, B_GROUP_W), BF16),
                   jax.ShapeDtypeStruct((bn, d, sub_len, B_GROUP_W), F32)),
        compiler_params=pltpu.CompilerParams(
            dimension_semantics=("parallel", "parallel"), vmem_limit_bytes=VMEM_LIMIT),
        name="attn_b",
    )(qkv, qkv, qkv, hmask, bias)


def _post_kernel(x_ref, ya_ref, o0_ref, l0_ref, o1_ref, l1_ref, o2_ref, l2_ref, gate_ref,
                 wa_ref, wb_ref, wo_ref, gmlp_ref, wup_ref, wdn_ref, gfin_ref,
                 y_ref, ril_ref, *, tn, ff_chunk, final_norm):
    dm = x_ref.shape[-1]
    n_slab = B_GROUP_W // LANES

    def natural_order(src_ref, d, slot):
        for r in range(d):
            blk = src_ref[0, r].astype(F32)
            for s in range(n_slab):
                ril_ref[slot * n_slab + s, pl.ds(r, tn // d, stride=d), :] = blk[:, s * LANES:(s + 1) * LANES]
        return jnp.concatenate([ril_ref[slot * n_slab + s] for s in range(n_slab)], axis=1)

    o0 = o0_ref[0, 0].astype(F32)
    l0 = l0_ref[0, 0]
    o1 = natural_order(o1_ref, B_DILATIONS[1], 0)
    l1 = natural_order(l1_ref, B_DILATIONS[1], 1)
    o2 = natural_order(o2_ref, B_DILATIONS[2], 2)
    l2 = natural_order(l2_ref, B_DILATIONS[2], 3)
    mx = jnp.maximum(jnp.maximum(l0, l1), l2)
    e0, e1, e2 = jnp.exp2(l0 - mx), jnp.exp2(l1 - mx), jnp.exp2(l2 - mx)
    yb_in = (e0 * o0 + e1 * o1 + e2 * o2) / (e0 + e1 + e2)

    ya = jnp.dot(ya_ref[0], wa_ref[...], preferred_element_type=F32)
    yb = jnp.dot(yb_in.astype(BF16), wb_ref[...], preferred_element_type=F32)
    ga = gate_ref[0, :, :dm].astype(F32)
    gb = gate_ref[0, :, dm:].astype(F32)
    mixed = ya / (1.0 + jnp.exp(-ga)) + yb / (1.0 + jnp.exp(-gb))
    x1 = x_ref[0] + jnp.dot(mixed.astype(BF16), wo_ref[...], preferred_element_type=F32)

    ms = jnp.mean(x1 * x1, axis=-1, keepdims=True)
    h2 = (x1 * lax.rsqrt(ms + EPS) * gmlp_ref[...]).astype(BF16)
    acc = x1
    d_ff = wup_ref.shape[1]
    for c in range(d_ff // ff_chunk):
        u = jnp.dot(h2, wup_ref[:, c * ff_chunk:(c + 1) * ff_chunk], preferred_element_type=F32)
        u = jnp.square(jnp.maximum(u, 0.0)).astype(BF16)
        acc = acc + jnp.dot(u, wdn_ref[c * ff_chunk:(c + 1) * ff_chunk, :], preferred_element_type=F32)
    if final_norm:
        ms = jnp.mean(acc * acc, axis=-1, keepdims=True)
        acc = acc * lax.rsqrt(ms + EPS) * gfin_ref[...]
    y_ref[0] = acc


def _post(x, ya, b_outs, gates, w_a_out, w_b_out, w_out, g_mlp, w_up, w_down, g_final,
          *, tn, final_norm, ff_chunk=512):
    bn, seq, dm = x.shape
    d_ff = w_up.shape[1]
    grid = (bn, seq // tn)
    in_specs = [
        pl.BlockSpec((1, tn, dm), lambda b, i: (b, i, 0)),
        pl.BlockSpec((1, tn, A_Q_W), lambda b, i: (b, i, 0)),
    ]
    operands = [x, ya]
    for d, (o, lse) in zip(B_DILATIONS, b_outs):
        spec = pl.BlockSpec((1, d, tn // d, B_GROUP_W), lambda b, i: (b, 0, i, 0))
        in_specs += [spec, spec]
        operands += [o, lse]
    in_specs += [
        pl.BlockSpec((1, tn, gates.shape[-1]), lambda b, i: (b, i, 0)),
        _const_spec(w_a_out.shape), _const_spec(w_b_out.shape), _const_spec(w_out.shape),
        _const_spec((1, dm)), _const_spec(w_up.shape), _const_spec(w_down.shape), _const_spec((1, dm)),
    ]
    operands += [gates, w_a_out, w_b_out, w_out, g_mlp, w_up, w_down, g_final]
    n_ril = 4 * (B_GROUP_W // LANES)
    return pl.pallas_call(
        functools.partial(_post_kernel, tn=tn, ff_chunk=ff_chunk, final_norm=final_norm),
        grid=grid,
        in_specs=in_specs,
        out_specs=pl.BlockSpec((1, tn, dm), lambda b, i: (b, i, 0)),
        out_shape=jax.ShapeDtypeStruct((bn, seq, dm), F32),
        scratch_shapes=[pltpu.VMEM((n_ril, tn, LANES), F32)],
        compiler_params=pltpu.CompilerParams(
            dimension_semantics=("parallel", "parallel"), vmem_limit_bytes=VMEM_LIMIT),
        name="post",
    )(*operands)


def _layer(x, w, tabs, *, final_norm, g_final):
    tab_a, tab_bq, tab_bk = tabs
    qa, ka, va, b0, b1, b2, gates = _inproj(
        x, w["w_in"], w["g_mix"], w["q_gain"], w["k_gain"], tab_a, tab_bq, tab_bk, tn=512)
    ya = _attn_a(qa, ka, va, tq=256, tk=512)
    b_outs = [_attn_b(b) for b in (b0, b1, b2)]
    return _post(x, ya, b_outs, gates, w["w_a_out"], w["w_b_out"], w["w_out"], w["g_mlp"],
                 w["w_up"], w["w_down"], g_final, tn=512, final_norm=final_norm)


def _trunk(x, layers, g_final):
    seq = x.shape[1]
    tab_a, tab_b = _rope_tables(seq)
    tabs = (tab_a, tab_b * Q_SCALE, tab_b)
    for i, w in enumerate(layers):
        x = _layer(x, w, tabs, final_norm=(i == len(layers) - 1), g_final=g_final)
    return x


def kernel(x_prompt, x_sample, w_in, w_a_out, w_b_out, w_out, g_mix, q_gain, k_gain, g_mlp,
           w_up, w_down, g_final):
    depth = w_in.shape[0]
    reps = LANES // HEAD_DIM
    layers = []
    for l in range(depth):
        layers.append(dict(
            w_in=w_in[l].astype(BF16), w_a_out=w_a_out[l].astype(BF16),
            w_b_out=w_b_out[l].astype(BF16), w_out=w_out[l].astype(BF16),
            w_up=w_up[l].astype(BF16), w_down=w_down[l].astype(BF16),
            g_mix=g_mix[l][None, :], g_mlp=g_mlp[l][None, :],
            q_gain=jnp.tile(q_gain[l] * Q_SCALE, reps)[None, :],
            k_gain=jnp.tile(k_gain[l], reps)[None, :],
        ))
    gf = g_final[None, :]
    return (_trunk(x_prompt, layers, gf), _trunk(x_sample, layers, gf))
```

```python
import functools
import math

import jax
import jax.numpy as jnp
import numpy as np
from jax import lax
from jax.experimental import pallas as pl
from jax.experimental.pallas import tpu as pltpu

F32 = jnp.float32
BF16 = jnp.bfloat16

HEAD_DIM = 64
A_Q_HEADS = 8
A_KV_HEADS = 2
B_DILATIONS = (1, 4, 16)
B_HALF_WINDOW = 64
B_HEADS_PER_GROUP = 4
GRID_W = 64
AXIAL_THETA = 10000.0
PARTIAL_THETA = 500000.0
PARTIAL_ROPE_DIM = HEAD_DIM // 4
EPS = 1e-6
MASK_VALUE = -1e30

LANES = 128
A_Q_W = A_Q_HEADS * HEAD_DIM
A_KV_W = A_KV_HEADS * HEAD_DIM
B_GROUP_W = B_HEADS_PER_GROUP * HEAD_DIM
B_W = len(B_DILATIONS) * B_GROUP_W
LOG2_E = math.log2(math.e)
Q_SCALE = HEAD_DIM ** -0.5 * LOG2_E

PROJ_CHUNK = 512
TILES_PER_STAGE = 4
VMEM_LIMIT = 56 * 1024 * 1024


def _const_spec(shape):
    return pl.BlockSpec(shape, lambda *_: (0,) * len(shape), pipeline_mode=pl.Buffered(1))


def _rope_tables(seq):
    t = np.arange(seq)

    def half_tables(pos, d, theta):
        d2 = d // 2
        freqs = theta ** (-(np.arange(d2, dtype=np.float64) * 2.0) / d)
        ang = pos.astype(np.float64)[:, None] * freqs[None, :]
        cos, sin = np.cos(ang), np.sin(ang)
        zero = np.zeros_like(sin)
        return (np.concatenate([cos, cos], -1), np.concatenate([-sin, zero], -1),
                np.concatenate([zero, sin], -1))

    half = HEAD_DIM // 2
    row = half_tables(t // GRID_W, half, AXIAL_THETA)
    col = half_tables(t % GRID_W, half, AXIAL_THETA)
    axial = [np.concatenate([r, c], -1) for r, c in zip(row, col)]
    part = half_tables(t, PARTIAL_ROPE_DIM, PARTIAL_THETA)
    rest = HEAD_DIM - PARTIAL_ROPE_DIM
    fill = (np.ones((seq, rest)), np.zeros((seq, rest)), np.zeros((seq, rest)))
    partial = [np.concatenate([p, f], -1) for p, f in zip(part, fill)]
    reps = LANES // HEAD_DIM
    tab_a = np.stack([np.tile(a, (1, reps)) for a in axial])
    tab_b = np.stack([np.tile(p, (1, reps)) for p in partial])
    return tab_a, tab_b


def _inproj_kernel(x_ref, gmix_ref, w_ref, qg_ref, kg_ref, seg_ref, ta_ref, tbq_ref, tbk_ref,
                   qa_ref, ka_ref, va_ref, b0_ref, b1_ref, b2_ref, gate_ref, z_ref, dl_ref, *, tn):
    x = x_ref[0]
    ms = jnp.mean(x * x, axis=-1, keepdims=True)
    h = (x * lax.rsqrt(ms + EPS) * gmix_ref[...]).astype(BF16)
    low = lax.broadcasted_iota(jnp.int32, (tn, LANES), 1) < HEAD_DIM
    n_slab = PROJ_CHUNK // LANES
    n_chunk = w_ref.shape[1] // PROJ_CHUNK
    first_gate = (A_Q_W + 2 * A_KV_W + 3 * B_W) // PROJ_CHUNK
    order = list(range(first_gate, n_chunk)) + list(range(first_gate))
    slot_of = {c: pos % 2 for pos, c in enumerate(order)}

    def stage(c):
        z = jnp.dot(h, w_ref[:, c * PROJ_CHUNK:(c + 1) * PROJ_CHUNK], preferred_element_type=F32)
        for s in range(n_slab):
            z_ref[slot_of[c], s] = z[:, s * LANES:(s + 1) * LANES]

    def slab(col):
        c, rem = divmod(col, PROJ_CHUNK)
        return z_ref.at[slot_of[c], rem // LANES]

    def inv_rms(zs, seg):
        z2 = jnp.concatenate([z * z for z in zs], axis=1).astype(BF16)
        ss = jnp.dot(z2, seg, preferred_element_type=F32)
        return lax.rsqrt(ss * (1.0 / HEAD_DIM) + EPS)

    def rope(z, tab_ref, sh):
        return (z * tab_ref[0] + pltpu.roll(z, LANES - sh, 1) * tab_ref[1]
                + pltpu.roll(z, sh, 1) * tab_ref[2])

    a_sh = HEAD_DIM // 4
    b_sh = PARTIAL_ROPE_DIM // 2
    kv_col = A_Q_W
    b_col = A_Q_W + 2 * A_KV_W
    g_col = b_col + 3 * B_W
    b_refs = (b0_ref, b1_ref, b2_ref)
    dl_slot = [0]

    def epilogue_a_q(col):
        zs = [slab(col)[...], slab(col + LANES)[...]]
        inv = inv_rms(zs, seg_ref[...])
        for i, z in enumerate(zs):
            zn = z * inv[:, i * LANES:(i + 1) * LANES] * qg_ref[...]
            c0 = col + i * LANES
            qa_ref[0, :, c0:c0 + LANES] = rope(zn, ta_ref, a_sh).astype(BF16)

    def epilogue_a_kv(col):
        zk = slab(col)[...]
        zk = rope(zk * inv_rms([zk], seg_ref[:LANES, :LANES]) * kg_ref[...], ta_ref, a_sh)
        zkr = pltpu.roll(zk, HEAD_DIM, 1)
        ka_ref[0, 0] = jnp.where(low, zk, zkr).astype(BF16)
        ka_ref[0, 1] = jnp.where(low, zkr, zk).astype(BF16)
        zv = slab(col + LANES)[...]
        va_ref[0, 0] = jnp.where(low, zv, 1.0).astype(BF16)
        va_ref[0, 1] = jnp.where(low, pltpu.roll(zv, HEAD_DIM, 1), 1.0).astype(BF16)

    def epilogue_b(col):
        t, rem = divmod(col - b_col, B_W)
        g, rem = divmod(rem, B_GROUP_W)
        d = B_DILATIONS[g]
        out_col = t * B_GROUP_W + rem
        tab_ref = (tbq_ref, tbk_ref, None)[t]
        src = slab(col)
        if tab_ref is not None:
            z = rope(src[...], tab_ref, b_sh)
            if d == 1:
                b_refs[g][0, 0, :, out_col:out_col + LANES] = z.astype(BF16)
                return
            src = dl_ref.at[dl_slot[0]]
            dl_slot[0] = 1 - dl_slot[0]
            src[...] = z
        elif d == 1:
            b_refs[g][0, 0, :, out_col:out_col + LANES] = src[...].astype(BF16)
            return
        for r in range(d):
            piece = src[pl.ds(r, tn // d, stride=d), :]
            b_refs[g][0, r, :, out_col:out_col + LANES] = piece.astype(BF16)

    def epilogue(c):
        col = c * PROJ_CHUNK
        while col < (c + 1) * PROJ_CHUNK:
            if col < kv_col:
                epilogue_a_q(col)
                col += 2 * LANES
            elif col < b_col:
                epilogue_a_kv(col)
                col += 2 * LANES
            elif col < g_col:
                epilogue_b(col)
                col += LANES
            else:
                gate = 1.0 / (1.0 + jnp.exp(-slab(col)[...]))
                gate_ref[0, :, col - g_col:col - g_col + LANES] = gate.astype(BF16)
                col += LANES

    stage(order[0])
    for pos, c in enumerate(order):
        if pos + 1 < n_chunk:
            stage(order[pos + 1])
        epilogue(c)


def _inproj(x, w_in, g_mix, q_gain, k_gain, tab_a, tab_bq, tab_bk, *, tn):
    bn, seq, dm = x.shape
    in_w = w_in.shape[1]
    gate_w = in_w - (A_Q_W + 2 * A_KV_W + 3 * B_W)
    grid = (bn, seq // tn)
    tab_spec = pl.BlockSpec((3, tn, LANES), lambda b, i: (0, i, 0))
    head_of = np.arange(2 * LANES) // HEAD_DIM
    seg = jnp.asarray(head_of[:, None] == head_of[None, :], BF16)
    out_shape = (
        jax.ShapeDtypeStruct((bn, seq, A_Q_W), BF16),
        jax.ShapeDtypeStruct((bn, A_KV_HEADS, seq, LANES), BF16),
        jax.ShapeDtypeStruct((bn, A_KV_HEADS, seq, LANES), BF16),
    ) + tuple(jax.ShapeDtypeStruct((bn, d, seq // d, 3 * B_GROUP_W), BF16) for d in B_DILATIONS) + (
        jax.ShapeDtypeStruct((bn, seq, gate_w), BF16),
    )
    out_specs = (
        pl.BlockSpec((1, tn, A_Q_W), lambda b, i: (b, i, 0)),
        pl.BlockSpec((1, A_KV_HEADS, tn, LANES), lambda b, i: (b, 0, i, 0)),
        pl.BlockSpec((1, A_KV_HEADS, tn, LANES), lambda b, i: (b, 0, i, 0)),
    ) + tuple(pl.BlockSpec((1, d, tn // d, 3 * B_GROUP_W), lambda b, i: (b, 0, i, 0))
              for d in B_DILATIONS) + (
        pl.BlockSpec((1, tn, gate_w), lambda b, i: (b, i, 0)),
    )
    return pl.pallas_call(
        functools.partial(_inproj_kernel, tn=tn),
        grid=grid,
        in_specs=[
            pl.BlockSpec((1, tn, dm), lambda b, i: (b, i, 0)),
            _const_spec((1, dm)),
            _const_spec((dm, in_w)),
            _const_spec((1, LANES)),
            _const_spec((1, LANES)),
            _const_spec(seg.shape),
            tab_spec, tab_spec, tab_spec,
        ],
        out_specs=out_specs,
        out_shape=out_shape,
        scratch_shapes=[pltpu.VMEM((2, PROJ_CHUNK // LANES, tn, LANES), F32),
                        pltpu.VMEM((2, tn, LANES), F32)],
        compiler_params=pltpu.CompilerParams(
            dimension_semantics=("parallel", "parallel"), vmem_limit_bytes=VMEM_LIMIT),
        name="inproj",
    )(x, g_mix, w_in, q_gain, k_gain, seg, tab_a, tab_bq, tab_bk)


def _attn_a_kernel(q_ref, k_ref, v_ref, o_ref, qs_ref, s_ref, pm_ref, m_ref, acc_ref,
                   *, tq, tk, nk):
    group = A_Q_HEADS // A_KV_HEADS
    n_lc = tk // LANES
    lane = lax.broadcasted_iota(jnp.int32, (tq, LANES), 1)
    low = lane < HEAD_DIM
    for h in range(group):
        pair = q_ref[0, :, (h // 2) * LANES:(h // 2 + 1) * LANES].astype(F32)
        keep = low if h % 2 == 0 else jnp.logical_not(low)
        qs_ref[h * tq:(h + 1) * tq, :] = jnp.where(keep, pair, 0.0).astype(BF16)
    m_ref[...] = jnp.full(m_ref.shape, MASK_VALUE, F32)
    acc_ref[...] = jnp.zeros(acc_ref.shape, F32)

    def scores(c):
        k = k_ref[0, 0, c * tk:(c + 1) * tk, :]
        s = lax.dot_general(qs_ref[...], k, (((1,), (1,)), ((), ())),
                            preferred_element_type=F32)
        s_ref[c % 2] = s
        pm = s[:, :LANES]
        for j in range(1, n_lc):
            pm = jnp.maximum(pm, s[:, j * LANES:(j + 1) * LANES])
        pm_ref[c % 2] = pm

    scores(0)
    for c in range(nk):
        if c + 1 < nk:
            scores(c + 1)
        slot = c % 2
        m_prev = m_ref[...]
        m_next = jnp.maximum(m_prev, jnp.max(pm_ref[slot], axis=1, keepdims=True))
        alpha = jnp.exp2(m_prev - m_next)
        m_ref[...] = m_next
        p = jnp.concatenate(
            [jnp.exp2(s_ref[slot, :, j * LANES:(j + 1) * LANES] - m_next) for j in range(n_lc)],
            axis=1).astype(BF16)
        v = v_ref[0, 0, c * tk:(c + 1) * tk, :]
        acc_ref[...] = alpha * acc_ref[...] + jnp.dot(p, v, preferred_element_type=F32)

    acc = acc_ref[...]
    acc_sw = pltpu.roll(acc, HEAD_DIM, 1)
    for pr in range(group // 2):
        ev = slice((2 * pr) * tq, (2 * pr + 1) * tq)
        od = slice((2 * pr + 1) * tq, (2 * pr + 2) * tq)
        out = jnp.where(low, acc[ev] / acc_sw[ev], acc_sw[od] / acc[od])
        o_ref[0, :, pr * LANES:(pr + 1) * LANES] = out.astype(BF16)


def _attn_a(qa, ka, va, *, tq, tk):
    bn, seq, _ = qa.shape
    group = A_Q_HEADS // A_KV_HEADS
    gw = group * HEAD_DIM
    grid = (bn, A_KV_HEADS, seq // tq)
    kv_spec = pl.BlockSpec((1, 1, seq, LANES), lambda b, j, i: (b, j, 0, 0))
    return pl.pallas_call(
        functools.partial(_attn_a_kernel, tq=tq, tk=tk, nk=seq // tk),
        grid=grid,
        in_specs=[pl.BlockSpec((1, tq, gw), lambda b, j, i: (b, i, j)), kv_spec, kv_spec],
        out_specs=pl.BlockSpec((1, tq, gw), lambda b, j, i: (b, i, j)),
        out_shape=jax.ShapeDtypeStruct((bn, seq, A_Q_W), BF16),
        scratch_shapes=[
            pltpu.VMEM((group * tq, LANES), BF16),
            pltpu.VMEM((2, group * tq, tk), F32),
            pltpu.VMEM((2, group * tq, LANES), F32),
            pltpu.VMEM((group * tq, LANES), F32),
            pltpu.VMEM((group * tq, LANES), F32),
        ],
        compiler_params=pltpu.CompilerParams(
            dimension_semantics=("parallel", "parallel", "parallel"), vmem_limit_bytes=VMEM_LIMIT),
        name="attn_a",
    )(qa, ka, va)


def _attn_b_kernel(q_ref, k_ref, v_ref, hmask_ref, bias_ref, o_ref, lse_ref,
                   s_scr, m_scr, *, sub_len, tq, tk, rb):
    nh = B_HEADS_PER_GROUP
    per_res = sub_len // tq
    n_tiles = rb * per_res
    low = lax.broadcasted_iota(jnp.int32, (tq, LANES), 1) < HEAD_DIM

    def locate(t):
        if isinstance(t, int):
            r, i = divmod(t, per_res)
            return r, i * tq, max(0, min(i * tq - B_HALF_WINDOW, sub_len - tk))
        r = t // per_res if rb > 1 else 0
        i0 = pl.multiple_of((t - r * per_res) * tq, tq)
        w0 = jnp.clip(i0 - B_HALF_WINDOW, 0, sub_len - tk)
        return r, i0, pl.multiple_of(w0, B_HALF_WINDOW)

    def lanes_to(x, width):
        return jnp.concatenate([x] * (width // LANES), axis=1)

    def stage_scores(t, slot):
        r, i0, w0 = locate(t)
        q = q_ref[0, r, pl.ds(i0, tq), :]
        k = k_ref[0, r, pl.ds(w0, tk), :]
        qs = jnp.concatenate([q * hmask_ref[h] for h in range(nh)], axis=0)
        s = lax.dot_general(qs, k, (((1,), (1,)), ((), ())), preferred_element_type=F32)
        bias = bias_ref[(i0 - w0) // B_HALF_WINDOW]
        s = s + jnp.concatenate([bias] * nh, axis=0)
        s_scr[slot] = s
        m_scr[slot] = jnp.broadcast_to(jnp.max(s, axis=1, keepdims=True), (nh * tq, LANES))

    def stage_finish(t, slot):
        r, i0, w0 = locate(t)
        v = v_ref[0, r, pl.ds(w0, tk), :]
        m = m_scr[slot]
        p = jnp.exp2(s_scr[slot] - lanes_to(m, tk))
        l = jnp.broadcast_to(jnp.sum(p, axis=1, keepdims=True), (nh * tq, LANES))
        lse = m + jnp.log(l) * LOG2_E
        pv = jnp.dot(p.astype(BF16), v, preferred_element_type=F32)
        for half in range(B_GROUP_W // LANES):
            lanes = slice(half * LANES, (half + 1) * LANES)
            ev = slice((2 * half) * tq, (2 * half + 1) * tq)
            od = slice((2 * half + 1) * tq, (2 * half + 2) * tq)
            num = jnp.where(low, pv[ev, lanes], pv[od, lanes])
            den = jnp.where(low, l[ev], l[od])
            o_ref[0, r, pl.ds(i0, tq), lanes] = (num / den).astype(BF16)
            lse_ref[0, r, pl.ds(i0, tq), lanes] = jnp.where(low, lse[ev], lse[od])

    gt = TILES_PER_STAGE
    n_groups = n_tiles // gt
    for u in range(gt):
        stage_scores(u, u)

    def step(g, carry):
        for u in range(gt):
            stage_finish(g * gt + u, u)
        for u in range(gt):
            stage_scores((g + 1) * gt + u, u)
        return carry

    lax.fori_loop(0, n_groups - 1, step, 0)
    for u in range(gt):
        stage_finish((n_groups - 1) * gt + u, u)


def _attn_b(qkv, *, tq=128, min_tiles=16):
    bn, d, sub_len, _ = qkv.shape
    tq = min(tq, sub_len)
    tk = min(tq + 2 * B_HALF_WINDOW, sub_len)
    per_res = sub_len // tq
    rb = min(d, max(1, min_tiles // per_res))
    assert (rb * per_res) % TILES_PER_STAGE == 0
    rows = B_HEADS_PER_GROUP * tq
    lane_head = np.arange(B_GROUP_W) // HEAD_DIM
    hmask = lane_head[None, None, :] == np.arange(B_HEADS_PER_GROUP)[:, None, None]
    hmask = jnp.asarray(np.broadcast_to(hmask, (B_HEADS_PER_GROUP, tq, B_GROUP_W)), BF16)
    rel = np.arange(tq)[:, None] - np.arange(tk)[None, :]
    offs = np.arange(3)[:, None, None] * B_HALF_WINDOW
    bias = jnp.asarray(np.where(np.abs(rel[None] + offs) <= B_HALF_WINDOW, 0.0, MASK_VALUE), F32)

    def spec(c):
        return pl.BlockSpec((1, rb, sub_len, B_GROUP_W), lambda b, r: (b, r, 0, c))

    out_spec = pl.BlockSpec((1, rb, sub_len, B_GROUP_W), lambda b, r: (b, r, 0, 0))
    return pl.pallas_call(
        functools.partial(_attn_b_kernel, sub_len=sub_len, tq=tq, tk=tk, rb=rb),
        grid=(bn, d // rb),
        in_specs=[spec(0), spec(1), spec(2), _const_spec(hmask.shape), _const_spec(bias.shape)],
        out_specs=(out_spec, out_spec),
        out_shape=(jax.ShapeDtypeStruct((bn, d, sub_len, B_GROUP_W), BF16),
                   jax.ShapeDtypeStruct((bn, d, sub_len, B_GROUP_W), F32)),
        scratch_shapes=[
            pltpu.VMEM((TILES_PER_STAGE, rows, tk), F32),
            pltpu.VMEM((TILES_PER_STAGE, rows, LANES), F32),
        ],
        compiler_params=pltpu.CompilerParams(
            dimension_semantics=("parallel", "parallel"), vmem_limit_bytes=VMEM_LIMIT),
        name="attn_b",
    )(qkv, qkv, qkv, hmask, bias)


def _post_kernel(x_ref, ya_ref, o0_ref, l0_ref, o1_ref, l1_ref, o2_ref, l2_ref, gate_ref,
                 wa_ref, wb_ref, wo_ref, gmlp_ref, wup_ref, wdn_ref, gfin_ref,
                 y_ref, ril_ref, *, tn, n_sub, ff_chunk, final_norm):
    dm = x_ref.shape[-1]
    n_slab = B_GROUP_W // LANES

    def to_natural_order(src_ref, d, slot):
        for r in range(d):
            blk = src_ref[0, r].astype(F32)
            for s in range(n_slab):
                ril_ref[slot * n_slab + s, pl.ds(r, tn // d, stride=d), :] = blk[:, s * LANES:(s + 1) * LANES]

    to_natural_order(o1_ref, B_DILATIONS[1], 0)
    to_natural_order(l1_ref, B_DILATIONS[1], 1)
    to_natural_order(o2_ref, B_DILATIONS[2], 2)
    to_natural_order(l2_ref, B_DILATIONS[2], 3)

    def natural(slot, rows):
        return jnp.concatenate([ril_ref[slot * n_slab + s, rows] for s in range(n_slab)], axis=1)

    def chain(rows):
        o0 = o0_ref[0, 0, rows].astype(F32)
        l0 = l0_ref[0, 0, rows]
        o1, l1, o2, l2 = (natural(slot, rows) for slot in range(4))
        mx = jnp.maximum(jnp.maximum(l0, l1), l2)
        e0, e1, e2 = jnp.exp2(l0 - mx), jnp.exp2(l1 - mx), jnp.exp2(l2 - mx)
        yb_in = (e0 * o0 + e1 * o1 + e2 * o2) / (e0 + e1 + e2)

        ya = jnp.dot(ya_ref[0, rows], wa_ref[...], preferred_element_type=F32)
        yb = jnp.dot(yb_in.astype(BF16), wb_ref[...], preferred_element_type=F32)
        ga = gate_ref[0, rows, :dm].astype(F32)
        gb = gate_ref[0, rows, dm:].astype(F32)
        mixed = ya * ga + yb * gb
        x1 = x_ref[0, rows] + jnp.dot(mixed.astype(BF16), wo_ref[...], preferred_element_type=F32)

        ms = jnp.mean(x1 * x1, axis=-1, keepdims=True)
        h2 = (x1 * lax.rsqrt(ms + EPS) * gmlp_ref[...]).astype(BF16)
        acc = x1
        d_ff = wup_ref.shape[1]
        for c in range(d_ff // ff_chunk):
            u = jnp.dot(h2, wup_ref[:, c * ff_chunk:(c + 1) * ff_chunk], preferred_element_type=F32)
            u = jnp.square(jnp.maximum(u, 0.0)).astype(BF16)
            acc = acc + jnp.dot(u, wdn_ref[c * ff_chunk:(c + 1) * ff_chunk, :],
                                preferred_element_type=F32)
        if final_norm:
            ms = jnp.mean(acc * acc, axis=-1, keepdims=True)
            acc = acc * lax.rsqrt(ms + EPS) * gfin_ref[...]
        y_ref[0, rows] = acc

    sub = tn // n_sub
    for i in range(n_sub):
        chain(slice(i * sub, (i + 1) * sub))


def _post(x, ya, b_outs, gates, w_a_out, w_b_out, w_out, g_mlp, w_up, w_down, g_final,
          *, tn, final_norm, n_sub=1, ff_chunk=512):
    bn, seq, dm = x.shape
    d_ff = w_up.shape[1]
    grid = (bn, seq // tn)
    in_specs = [
        pl.BlockSpec((1, tn, dm), lambda b, i: (b, i, 0)),
        pl.BlockSpec((1, tn, A_Q_W), lambda b, i: (b, i, 0)),
    ]
    operands = [x, ya]
    for d, (o, lse) in zip(B_DILATIONS, b_outs):
        spec = pl.BlockSpec((1, d, tn // d, B_GROUP_W), lambda b, i: (b, 0, i, 0))
        in_specs += [spec, spec]
        operands += [o, lse]
    in_specs += [
        pl.BlockSpec((1, tn, gates.shape[-1]), lambda b, i: (b, i, 0)),
        _const_spec(w_a_out.shape), _const_spec(w_b_out.shape), _const_spec(w_out.shape),
        _const_spec((1, dm)), _const_spec(w_up.shape), _const_spec(w_down.shape), _const_spec((1, dm)),
    ]
    operands += [gates, w_a_out, w_b_out, w_out, g_mlp, w_up, w_down, g_final]
    n_ril = 4 * (B_GROUP_W // LANES)
    return pl.pallas_call(
        functools.partial(_post_kernel, tn=tn, n_sub=n_sub, ff_chunk=ff_chunk,
                          final_norm=final_norm),
        grid=grid,
        in_specs=in_specs,
        out_specs=pl.BlockSpec((1, tn, dm), lambda b, i: (b, i, 0)),
        out_shape=jax.ShapeDtypeStruct((bn, seq, dm), F32),
        scratch_shapes=[pltpu.VMEM((n_ril, tn, LANES), F32)],
        compiler_params=pltpu.CompilerParams(
            dimension_semantics=("parallel", "parallel"), vmem_limit_bytes=VMEM_LIMIT),
        name="post",
    )(*operands)


def _layer(x, w, tabs, *, final_norm, g_final):
    tab_a, tab_bq, tab_bk = tabs
    qa, ka, va, b0, b1, b2, gates = _inproj(
        x, w["w_in"], w["g_mix"], w["q_gain"], w["k_gain"], tab_a, tab_bq, tab_bk, tn=512)
    ya = _attn_a(qa, ka, va, tq=512, tk=512)
    b_outs = [_attn_b(b) for b in (b0, b1, b2)]
    return _post(x, ya, b_outs, gates, w["w_a_out"], w["w_b_out"], w["w_out"], w["g_mlp"],
                 w["w_up"], w["w_down"], g_final, tn=512, final_norm=final_norm)


def _trunk(x, layers, g_final):
    seq = x.shape[1]
    tab_a, tab_b = _rope_tables(seq)
    tabs = tuple(jnp.asarray(t, F32) for t in (tab_a, tab_b * Q_SCALE, tab_b))
    for i, w in enumerate(layers):
        x = _layer(x, w, tabs, final_norm=(i == len(layers) - 1), g_final=g_final)
    return x


def kernel(x_prompt, x_sample, w_in, w_a_out, w_b_out, w_out, g_mix, q_gain, k_gain, g_mlp,
           w_up, w_down, g_final):
    depth = w_in.shape[0]
    reps = LANES // HEAD_DIM
    layers = []
    for l in range(depth):
        layers.append(dict(
            w_in=w_in[l].astype(BF16), w_a_out=w_a_out[l].astype(BF16),
            w_b_out=w_b_out[l].astype(BF16), w_out=w_out[l].astype(BF16),
            w_up=w_up[l].astype(BF16), w_down=w_down[l].astype(BF16),
            g_mix=g_mix[l][None, :], g_mlp=g_mlp[l][None, :],
            q_gain=jnp.tile(q_gain[l] * Q_SCALE, reps)[None, :],
            k_gain=jnp.tile(k_gain[l], reps)[None, :],
        ))
    gf = g_final[None, :]
    return (_trunk(x_prompt, layers, gf), _trunk(x_sample, layers, gf))
```

```python
import functools
import math

import jax
import jax.numpy as jnp
import numpy as np
from jax import lax
from jax.experimental import pallas as pl
from jax.experimental.pallas import tpu as pltpu

F32 = jnp.float32
BF16 = jnp.bfloat16

HEAD_DIM = 64
A_Q_HEADS = 8
A_KV_HEADS = 2
B_DILATIONS = (1, 4, 16)
B_HALF_WINDOW = 64
B_HEADS_PER_GROUP = 4
GRID_W = 64
AXIAL_THETA = 10000.0
PARTIAL_THETA = 500000.0
PARTIAL_ROPE_DIM = HEAD_DIM // 4
EPS = 1e-6
MASK_VALUE = -1e30

LANES = 128
SUBLANES = 8
MXU_WIDTH = 256
A_Q_W = A_Q_HEADS * HEAD_DIM
A_KV_W = A_KV_HEADS * HEAD_DIM
B_GROUP_W = B_HEADS_PER_GROUP * HEAD_DIM
B_W = len(B_DILATIONS) * B_GROUP_W
LOG2_E = math.log2(math.e)
Q_SCALE = HEAD_DIM ** -0.5 * LOG2_E

PROJ_CHUNK = 512
TILES_PER_STAGE = 4
VMEM_LIMIT = 56 * 1024 * 1024


def _const_spec(shape):
    return pl.BlockSpec(shape, lambda *_: (0,) * len(shape), pipeline_mode=pl.Buffered(1))


def _rope_tables(seq):
    t = np.arange(seq)

    def half_tables(pos, d, theta):
        d2 = d // 2
        freqs = theta ** (-(np.arange(d2, dtype=np.float64) * 2.0) / d)
        ang = pos.astype(np.float64)[:, None] * freqs[None, :]
        cos, sin = np.cos(ang), np.sin(ang)
        zero = np.zeros_like(sin)
        return (np.concatenate([cos, cos], -1), np.concatenate([-sin, zero], -1),
                np.concatenate([zero, sin], -1))

    half = HEAD_DIM // 2
    row = half_tables(t // GRID_W, half, AXIAL_THETA)
    col = half_tables(t % GRID_W, half, AXIAL_THETA)
    axial = [np.concatenate([r, c], -1) for r, c in zip(row, col)]
    part = half_tables(t, PARTIAL_ROPE_DIM, PARTIAL_THETA)
    rest = HEAD_DIM - PARTIAL_ROPE_DIM
    fill = (np.ones((seq, rest)), np.zeros((seq, rest)), np.zeros((seq, rest)))
    partial = [np.concatenate([p, f], -1) for p, f in zip(part, fill)]
    reps = LANES // HEAD_DIM
    tab_a = np.stack([np.tile(a, (1, reps)) for a in axial])
    tab_b = np.stack([np.tile(p, (1, reps)) for p in partial])
    return tab_a, tab_b


def _inproj_kernel(x_ref, gmix_ref, w_ref, qg_ref, kg_ref, seg_ref, ta_ref, tbq_ref, tbk_ref,
                   qa_ref, ka_ref, va_ref, b0_ref, b1_ref, b2_ref, gate_ref, z_ref, dl_ref, *, tn):
    x = x_ref[0]
    ms = jnp.mean(x * x, axis=-1, keepdims=True)
    h = (x * lax.rsqrt(ms + EPS) * gmix_ref[...]).astype(BF16)
    low = lax.broadcasted_iota(jnp.int32, (tn, LANES), 1) < HEAD_DIM
    n_slab = PROJ_CHUNK // LANES
    n_chunk = w_ref.shape[1] // PROJ_CHUNK
    first_gate = (A_Q_W + 2 * A_KV_W + 3 * B_W) // PROJ_CHUNK
    order = list(range(first_gate, n_chunk)) + list(range(first_gate))
    slot_of = {c: pos % 2 for pos, c in enumerate(order)}

    def stage(c):
        z = jnp.dot(h, w_ref[:, c * PROJ_CHUNK:(c + 1) * PROJ_CHUNK], preferred_element_type=F32)
        for s in range(n_slab):
            z_ref[slot_of[c], s] = z[:, s * LANES:(s + 1) * LANES]

    def slab(col):
        c, rem = divmod(col, PROJ_CHUNK)
        return z_ref.at[slot_of[c], rem // LANES]

    def inv_rms(zs, seg):
        z2 = jnp.concatenate([z * z for z in zs], axis=1).astype(BF16)
        ss = jnp.dot(z2, seg, preferred_element_type=F32)
        return lax.rsqrt(ss * (1.0 / HEAD_DIM) + EPS)

    def rope(z, tab_ref, sh):
        return (z * tab_ref[0] + pltpu.roll(z, LANES - sh, 1) * tab_ref[1]
                + pltpu.roll(z, sh, 1) * tab_ref[2])

    a_sh = HEAD_DIM // 4
    b_sh = PARTIAL_ROPE_DIM // 2
    kv_col = A_Q_W
    b_col = A_Q_W + 2 * A_KV_W
    g_col = b_col + 3 * B_W
    b_refs = (b0_ref, b1_ref, b2_ref)
    dl_slot = [0]

    def epilogue_a_q(col):
        zs = [slab(col)[...], slab(col + LANES)[...]]
        inv = inv_rms(zs, seg_ref[...])
        for i, z in enumerate(zs):
            zn = z * inv[:, i * LANES:(i + 1) * LANES] * qg_ref[...]
            c0 = col + i * LANES
            qa_ref[0, :, c0:c0 + LANES] = rope(zn, ta_ref, a_sh).astype(BF16)

    def epilogue_a_kv(col):
        zk = slab(col)[...]
        zk = rope(zk * inv_rms([zk], seg_ref[:LANES, :LANES]) * kg_ref[...], ta_ref, a_sh)
        zkr = pltpu.roll(zk, HEAD_DIM, 1)
        ka_ref[0, 0] = jnp.where(low, zk, zkr).astype(BF16)
        ka_ref[0, 1] = jnp.where(low, zkr, zk).astype(BF16)
        zvt = slab(col + LANES)[...].T
        ones = jnp.ones((HEAD_DIM, tn), F32)
        for j in range(A_KV_HEADS):
            vt = jnp.concatenate([zvt[j * HEAD_DIM:(j + 1) * HEAD_DIM], ones], axis=0)
            va_ref[0, j] = vt.astype(BF16)

    def epilogue_b(col):
        t, rem = divmod(col - b_col, B_W)
        g, rem = divmod(rem, B_GROUP_W)
        d = B_DILATIONS[g]
        out_col = t * B_GROUP_W + rem
        tab_ref = (tbq_ref, tbk_ref, None)[t]
        src = slab(col)
        if tab_ref is not None:
            z = rope(src[...], tab_ref, b_sh)
            if d == 1:
                b_refs[g][0, 0, :, out_col:out_col + LANES] = z.astype(BF16)
                return
            src = dl_ref.at[dl_slot[0]]
            dl_slot[0] = 1 - dl_slot[0]
            src[...] = z
        elif d == 1:
            b_refs[g][0, 0, :, out_col:out_col + LANES] = src[...].astype(BF16)
            return
        for r in range(d):
            piece = src[pl.ds(r, tn // d, stride=d), :]
            b_refs[g][0, r, :, out_col:out_col + LANES] = piece.astype(BF16)

    def epilogue(c):
        col = c * PROJ_CHUNK
        while col < (c + 1) * PROJ_CHUNK:
            if col < kv_col:
                epilogue_a_q(col)
                col += 2 * LANES
            elif col < b_col:
                epilogue_a_kv(col)
                col += 2 * LANES
            elif col < g_col:
                epilogue_b(col)
                col += LANES
            else:
                gate = 1.0 / (1.0 + jnp.exp(-slab(col)[...]))
                gate_ref[0, :, col - g_col:col - g_col + LANES] = gate.astype(BF16)
                col += LANES

    stage(order[0])
    for pos, c in enumerate(order):
        if pos + 1 < n_chunk:
            stage(order[pos + 1])
        epilogue(c)


def _inproj(x, w_in, g_mix, q_gain, k_gain, tab_a, tab_bq, tab_bk, *, tn):
    bn, seq, dm = x.shape
    in_w = w_in.shape[1]
    gate_w = in_w - (A_Q_W + 2 * A_KV_W + 3 * B_W)
    grid = (bn, seq // tn)
    tab_spec = pl.BlockSpec((3, tn, LANES), lambda b, i: (0, i, 0))
    head_of = np.arange(2 * LANES) // HEAD_DIM
    seg = jnp.asarray(head_of[:, None] == head_of[None, :], BF16)
    out_shape = (
        jax.ShapeDtypeStruct((bn, seq, A_Q_W), BF16),
        jax.ShapeDtypeStruct((bn, A_KV_HEADS, seq, LANES), BF16),
        jax.ShapeDtypeStruct((bn, A_KV_HEADS, LANES, seq), BF16),
    ) + tuple(jax.ShapeDtypeStruct((bn, d, seq // d, 3 * B_GROUP_W), BF16) for d in B_DILATIONS) + (
        jax.ShapeDtypeStruct((bn, seq, gate_w), BF16),
    )
    out_specs = (
        pl.BlockSpec((1, tn, A_Q_W), lambda b, i: (b, i, 0)),
        pl.BlockSpec((1, A_KV_HEADS, tn, LANES), lambda b, i: (b, 0, i, 0)),
        pl.BlockSpec((1, A_KV_HEADS, LANES, tn), lambda b, i: (b, 0, 0, i)),
    ) + tuple(pl.BlockSpec((1, d, tn // d, 3 * B_GROUP_W), lambda b, i: (b, 0, i, 0))
              for d in B_DILATIONS) + (
        pl.BlockSpec((1, tn, gate_w), lambda b, i: (b, i, 0)),
    )
    return pl.pallas_call(
        functools.partial(_inproj_kernel, tn=tn),
        grid=grid,
        in_specs=[
            pl.BlockSpec((1, tn, dm), lambda b, i: (b, i, 0)),
            _const_spec((1, dm)),
            _const_spec((dm, in_w)),
            _const_spec((1, LANES)),
            _const_spec((1, LANES)),
            _const_spec(seg.shape),
            tab_spec, tab_spec, tab_spec,
        ],
        out_specs=out_specs,
        out_shape=out_shape,
        scratch_shapes=[pltpu.VMEM((2, PROJ_CHUNK // LANES, tn, LANES), F32),
                        pltpu.VMEM((2, tn, LANES), F32)],
        compiler_params=pltpu.CompilerParams(
            dimension_semantics=("parallel", "parallel"), vmem_limit_bytes=VMEM_LIMIT),
        name="inproj",
    )(x, g_mix, w_in, q_gain, k_gain, seg, tab_a, tab_bq, tab_bk)


def _attn_a_kernel(q_ref, k_ref, vt_ref, o_ref, qs_ref, s_ref, pm_ref, m_ref, acc_ref,
                   *, tq, tk, nk):
    group = A_Q_HEADS // A_KV_HEADS
    rows = group * tq
    top = lax.broadcasted_iota(jnp.int32, (LANES, tq), 0) < HEAD_DIM
    for pr in range(group // 2):
        pair_t = q_ref[0, :, pr * LANES:(pr + 1) * LANES].astype(F32).T
        qs_ref[:, (2 * pr) * tq:(2 * pr + 1) * tq] = jnp.where(top, pair_t, 0.0).astype(BF16)
        qs_ref[:, (2 * pr + 1) * tq:(2 * pr + 2) * tq] = jnp.where(top, 0.0, pair_t).astype(BF16)
    m_ref[...] = jnp.full(m_ref.shape, MASK_VALUE, F32)
    acc_ref[...] = jnp.zeros(acc_ref.shape, F32)

    nb = MXU_WIDTH
    blocks = [slice(i * nb, (i + 1) * nb) for i in range(rows // nb)]

    def scores(c):
        k = k_ref[0, 0, c * tk:(c + 1) * tk, :]
        for blk in blocks:
            s = jnp.dot(k, qs_ref[:, blk], preferred_element_type=F32)
            s_ref[c % 2, :, blk] = s
            pm_ref[c % 2, :, blk] = jnp.max(s.reshape(tk // SUBLANES, SUBLANES, nb), axis=0)

    scores(0)
    for c in range(nk):
        if c + 1 < nk:
            scores(c + 1)
        slot = c % 2
        vt = vt_ref[0, 0, :, c * tk:(c + 1) * tk]
        for blk in blocks:
            m_prev = m_ref[:, blk]
            m_next = jnp.maximum(m_prev, jnp.max(pm_ref[slot, :, blk], axis=0, keepdims=True))
            alpha = jnp.exp2(m_prev - m_next)
            m_ref[:, blk] = m_next
            s3 = s_ref[slot, :, blk].reshape(tk // SUBLANES, SUBLANES, nb)
            p = jnp.exp2(s3 - m_next[None]).reshape(tk, nb).astype(BF16)
            pv = jnp.dot(vt, p, preferred_element_type=F32)
            acc3 = acc_ref[:, blk].reshape(LANES // SUBLANES, SUBLANES, nb)
            acc_ref[:, blk] = (acc3 * alpha[None]).reshape(LANES, nb) + pv

    acc = acc_ref[...]
    for pr in range(group // 2):
        ev = slice((2 * pr) * tq, (2 * pr + 1) * tq)
        od = slice((2 * pr + 1) * tq, (2 * pr + 2) * tq)
        num = jnp.concatenate([acc[:HEAD_DIM, ev], acc[:HEAD_DIM, od]], axis=0)
        den = jnp.concatenate([acc[HEAD_DIM:, ev], acc[HEAD_DIM:, od]], axis=0)
        o_ref[0, :, pr * LANES:(pr + 1) * LANES] = (num / den).T.astype(BF16)


def _attn_a(qa, ka, vat, *, tq, tk):
    bn, seq, _ = qa.shape
    group = A_Q_HEADS // A_KV_HEADS
    gw = group * HEAD_DIM
    rows = group * tq
    grid = (bn, A_KV_HEADS, seq // tq)
    return pl.pallas_call(
        functools.partial(_attn_a_kernel, tq=tq, tk=tk, nk=seq // tk),
        grid=grid,
        in_specs=[pl.BlockSpec((1, tq, gw), lambda b, j, i: (b, i, j)),
                  pl.BlockSpec((1, 1, seq, LANES), lambda b, j, i: (b, j, 0, 0)),
                  pl.BlockSpec((1, 1, LANES, seq), lambda b, j, i: (b, j, 0, 0))],
        out_specs=pl.BlockSpec((1, tq, gw), lambda b, j, i: (b, i, j)),
        out_shape=jax.ShapeDtypeStruct((bn, seq, A_Q_W), BF16),
        scratch_shapes=[
            pltpu.VMEM((LANES, rows), BF16),
            pltpu.VMEM((2, tk, rows), F32),
            pltpu.VMEM((2, SUBLANES, rows), F32),
            pltpu.VMEM((SUBLANES, rows), F32),
            pltpu.VMEM((LANES, rows), F32),
        ],
        compiler_params=pltpu.CompilerParams(
            dimension_semantics=("parallel", "parallel", "parallel"), vmem_limit_bytes=VMEM_LIMIT),
        name="attn_a",
    )(qa, ka, vat)


def _attn_b_kernel(q_ref, k_ref, v_ref, hmask_ref, bias_ref, o_ref, lse_ref,
                   s_scr, m_scr, *, sub_len, tq, tk, rb):
    nh = B_HEADS_PER_GROUP
    per_res = sub_len // tq
    n_tiles = rb * per_res
    low = lax.broadcasted_iota(jnp.int32, (tq, LANES), 1) < HEAD_DIM

    def locate(t):
        if isinstance(t, int):
            r, i = divmod(t, per_res)
            return r, i * tq, max(0, min(i * tq - B_HALF_WINDOW, sub_len - tk))
        r = t // per_res if rb > 1 else 0
        i0 = pl.multiple_of((t - r * per_res) * tq, tq)
        w0 = jnp.clip(i0 - B_HALF_WINDOW, 0, sub_len - tk)
        return r, i0, pl.multiple_of(w0, B_HALF_WINDOW)

    def lanes_to(x, width):
        return jnp.concatenate([x] * (width // LANES), axis=1)

    def stage_scores(t, slot):
        r, i0, w0 = locate(t)
        q = q_ref[0, r, pl.ds(i0, tq), :]
        k = k_ref[0, r, pl.ds(w0, tk), :]
        qs = jnp.concatenate([q * hmask_ref[h] for h in range(nh)], axis=0)
        s = lax.dot_general(qs, k, (((1,), (1,)), ((), ())), preferred_element_type=F32)
        bias = bias_ref[(i0 - w0) // B_HALF_WINDOW]
        s = s + jnp.concatenate([bias] * nh, axis=0)
        s_scr[slot] = s
        m_scr[slot] = jnp.broadcast_to(jnp.max(s, axis=1, keepdims=True), (nh * tq, LANES))

    def stage_finish(t, slot):
        r, i0, w0 = locate(t)
        v = v_ref[0, r, pl.ds(w0, tk), :]
        m = m_scr[slot]
        p = jnp.exp2(s_scr[slot] - lanes_to(m, tk))
        l = jnp.broadcast_to(jnp.sum(p, axis=1, keepdims=True), (nh * tq, LANES))
        lse = m + jnp.log(l) * LOG2_E
        pv = jnp.dot(p.astype(BF16), v, preferred_element_type=F32)
        for half in range(B_GROUP_W // LANES):
            lanes = slice(half * LANES, (half + 1) * LANES)
            ev = slice((2 * half) * tq, (2 * half + 1) * tq)
            od = slice((2 * half + 1) * tq, (2 * half + 2) * tq)
            num = jnp.where(low, pv[ev, lanes], pv[od, lanes])
            den = jnp.where(low, l[ev], l[od])
            o_ref[0, r, pl.ds(i0, tq), lanes] = (num / den).astype(BF16)
            lse_ref[0, r, pl.ds(i0, tq), lanes] = jnp.where(low, lse[ev], lse[od])

    gt = TILES_PER_STAGE
    n_groups = n_tiles // gt
    for u in range(gt):
        stage_scores(u, u)

    def step(g, carry):
        for u in range(gt):
            stage_finish(g * gt + u, u)
        for u in range(gt):
            stage_scores((g + 1) * gt + u, u)
        return carry

    lax.fori_loop(0, n_groups - 1, step, 0)
    for u in range(gt):
        stage_finish((n_groups - 1) * gt + u, u)


def _attn_b(qkv, *, tq=128, min_tiles=16):
    bn, d, sub_len, _ = qkv.shape
    tq = min(tq, sub_len)
    tk = min(tq + 2 * B_HALF_WINDOW, sub_len)
    per_res = sub_len // tq
    rb = min(d, max(1, min_tiles // per_res))
    assert (rb * per_res) % TILES_PER_STAGE == 0
    rows = B_HEADS_PER_GROUP * tq
    lane_head = np.arange(B_GROUP_W) // HEAD_DIM
    hmask = lane_head[None, None, :] == np.arange(B_HEADS_PER_GROUP)[:, None, None]
    hmask = jnp.asarray(np.broadcast_to(hmask, (B_HEADS_PER_GROUP, tq, B_GROUP_W)), BF16)
    rel = np.arange(tq)[:, None] - np.arange(tk)[None, :]
    offs = np.arange(3)[:, None, None] * B_HALF_WINDOW
    bias = jnp.asarray(np.where(np.abs(rel[None] + offs) <= B_HALF_WINDOW, 0.0, MASK_VALUE), F32)

    def spec(c):
        return pl.BlockSpec((1, rb, sub_len, B_GROUP_W), lambda b, r: (b, r, 0, c))

    out_spec = pl.BlockSpec((1, rb, sub_len, B_GROUP_W), lambda b, r: (b, r, 0, 0))
    return pl.pallas_call(
        functools.partial(_attn_b_kernel, sub_len=sub_len, tq=tq, tk=tk, rb=rb),
        grid=(bn, d // rb),
        in_specs=[spec(0), spec(1), spec(2), _const_spec(hmask.shape), _const_spec(bias.shape)],
        out_specs=(out_spec, out_spec),
        out_shape=(jax.ShapeDtypeStruct((bn, d, sub_len, B_GROUP_W), BF16),
                   jax.ShapeDtypeStruct((bn, d, sub_len, B_GROUP_W), F32)),
        scratch_shapes=[
            pltpu.VMEM((TILES_PER_STAGE, rows, tk), F32),
            pltpu.VMEM((TILES_PER_STAGE, rows, LANES), F32),
        ],
        compiler_params=pltpu.CompilerParams(
            dimension_semantics=("parallel", "parallel"), vmem_limit_bytes=VMEM_LIMIT),
        name="attn_b",
    )(qkv, qkv, qkv, hmask, bias)


def _post_kernel(x_ref, ya_ref, o0_ref, l0_ref, o1_ref, l1_ref, o2_ref, l2_ref, gate_ref,
                 wa_ref, wb_ref, wo_ref, gmlp_ref, wup_ref, wdn_ref, gfin_ref,
                 y_ref, ril_ref, *, tn, n_sub, ff_chunk, final_norm):
    dm = x_ref.shape[-1]
    n_slab = B_GROUP_W // LANES

    def to_natural_order(src_ref, d, slot):
        for r in range(d):
            blk = src_ref[0, r].astype(F32)
            for s in range(n_slab):
                ril_ref[slot * n_slab + s, pl.ds(r, tn // d, stride=d), :] = blk[:, s * LANES:(s + 1) * LANES]

    to_natural_order(o1_ref, B_DILATIONS[1], 0)
    to_natural_order(l1_ref, B_DILATIONS[1], 1)
    to_natural_order(o2_ref, B_DILATIONS[2], 2)
    to_natural_order(l2_ref, B_DILATIONS[2], 3)

    def natural(slot, rows):
        return jnp.concatenate([ril_ref[slot * n_slab + s, rows] for s in range(n_slab)], axis=1)

    def chain(rows):
        o0 = o0_ref[0, 0, rows].astype(F32)
        l0 = l0_ref[0, 0, rows]
        o1, l1, o2, l2 = (natural(slot, rows) for slot in range(4))
        mx = jnp.maximum(jnp.maximum(l0, l1), l2)
        e0, e1, e2 = jnp.exp2(l0 - mx), jnp.exp2(l1 - mx), jnp.exp2(l2 - mx)
        yb_in = (e0 * o0 + e1 * o1 + e2 * o2) / (e0 + e1 + e2)

        ya = jnp.dot(ya_ref[0, rows], wa_ref[...], preferred_element_type=F32)
        yb = jnp.dot(yb_in.astype(BF16), wb_ref[...], preferred_element_type=F32)
        ga = gate_ref[0, rows, :dm].astype(F32)
        gb = gate_ref[0, rows, dm:].astype(F32)
        mixed = ya * ga + yb * gb
        x1 = x_ref[0, rows] + jnp.dot(mixed.astype(BF16), wo_ref[...], preferred_element_type=F32)

        ms = jnp.mean(x1 * x1, axis=-1, keepdims=True)
        h2 = (x1 * lax.rsqrt(ms + EPS) * gmlp_ref[...]).astype(BF16)
        acc = x1
        d_ff = wup_ref.shape[1]
        for c in range(d_ff // ff_chunk):
            u = jnp.dot(h2, wup_ref[:, c * ff_chunk:(c + 1) * ff_chunk], preferred_element_type=F32)
            u = jnp.square(jnp.maximum(u, 0.0)).astype(BF16)
            acc = acc + jnp.dot(u, wdn_ref[c * ff_chunk:(c + 1) * ff_chunk, :],
                                preferred_element_type=F32)
        if final_norm:
            ms = jnp.mean(acc * acc, axis=-1, keepdims=True)
            acc = acc * lax.rsqrt(ms + EPS) * gfin_ref[...]
        y_ref[0, rows] = acc

    sub = tn // n_sub
    for i in range(n_sub):
        chain(slice(i * sub, (i + 1) * sub))


def _post(x, ya, b_outs, gates, w_a_out, w_b_out, w_out, g_mlp, w_up, w_down, g_final,
          *, tn, final_norm, n_sub=1, ff_chunk=512):
    bn, seq, dm = x.shape
    d_ff = w_up.shape[1]
    grid = (bn, seq // tn)
    in_specs = [
        pl.BlockSpec((1, tn, dm), lambda b, i: (b, i, 0)),
        pl.BlockSpec((1, tn, A_Q_W), lambda b, i: (b, i, 0)),
    ]
    operands = [x, ya]
    for d, (o, lse) in zip(B_DILATIONS, b_outs):
        spec = pl.BlockSpec((1, d, tn // d, B_GROUP_W), lambda b, i: (b, 0, i, 0))
        in_specs += [spec, spec]
        operands += [o, lse]
    in_specs += [
        pl.BlockSpec((1, tn, gates.shape[-1]), lambda b, i: (b, i, 0)),
        _const_spec(w_a_out.shape), _const_spec(w_b_out.shape), _const_spec(w_out.shape),
        _const_spec((1, dm)), _const_spec(w_up.shape), _const_spec(w_down.shape), _const_spec((1, dm)),
    ]
    operands += [gates, w_a_out, w_b_out, w_out, g_mlp, w_up, w_down, g_final]
    n_ril = 4 * (B_GROUP_W // LANES)
    return pl.pallas_call(
        functools.partial(_post_kernel, tn=tn, n_sub=n_sub, ff_chunk=ff_chunk,
                          final_norm=final_norm),
        grid=grid,
        in_specs=in_specs,
        out_specs=pl.BlockSpec((1, tn, dm), lambda b, i: (b, i, 0)),
        out_shape=jax.ShapeDtypeStruct((bn, seq, dm), F32),
        scratch_shapes=[pltpu.VMEM((n_ril, tn, LANES), F32)],
        compiler_params=pltpu.CompilerParams(
            dimension_semantics=("parallel", "parallel"), vmem_limit_bytes=VMEM_LIMIT),
        name="post",
    )(*operands)


def _layer(x, w, tabs, *, final_norm, g_final):
    tab_a, tab_bq, tab_bk = tabs
    qa, ka, va, b0, b1, b2, gates = _inproj(
        x, w["w_in"], w["g_mix"], w["q_gain"], w["k_gain"], tab_a, tab_bq, tab_bk, tn=512)
    ya = _attn_a(qa, ka, va, tq=512, tk=1024)
    b_outs = [_attn_b(b) for b in (b0, b1, b2)]
    return _post(x, ya, b_outs, gates, w["w_a_out"], w["w_b_out"], w["w_out"], w["g_mlp"],
                 w["w_up"], w["w_down"], g_final, tn=512, final_norm=final_norm)


def _trunk(x, layers, g_final):
    seq = x.shape[1]
    tab_a, tab_b = _rope_tables(seq)
    tabs = tuple(jnp.asarray(t, F32) for t in (tab_a, tab_b * Q_SCALE, tab_b))
    for i, w in enumerate(layers):
        x = _layer(x, w, tabs, final_norm=(i == len(layers) - 1), g_final=g_final)
    return x


def kernel(x_prompt, x_sample, w_in, w_a_out, w_b_out, w_out, g_mix, q_gain, k_gain, g_mlp,
           w_up, w_down, g_final):
    depth = w_in.shape[0]
    reps = LANES // HEAD_DIM
    layers = []
    for l in range(depth):
        layers.append(dict(
            w_in=w_in[l].astype(BF16), w_a_out=w_a_out[l].astype(BF16),
            w_b_out=w_b_out[l].astype(BF16), w_out=w_out[l].astype(BF16),
            w_up=w_up[l].astype(BF16), w_down=w_down[l].astype(BF16),
            g_mix=g_mix[l][None, :], g_mlp=g_mlp[l][None, :],
            q_gain=jnp.tile(q_gain[l] * Q_SCALE, reps)[None, :],
            k_gain=jnp.tile(k_gain[l], reps)[None, :],
        ))
    gf = g_final[None, :]
    return (_trunk(x_prompt, layers, gf), _trunk(x_sample, layers, gf))
```

```python
import functools
import math

import jax
import jax.numpy as jnp
import numpy as np
from jax import lax
from jax.experimental import pallas as pl
from jax.experimental.pallas import tpu as pltpu

F32 = jnp.float32
BF16 = jnp.bfloat16

HEAD_DIM = 64
A_Q_HEADS = 8
A_KV_HEADS = 2
B_DILATIONS = (1, 4, 16)
B_HALF_WINDOW = 64
B_HEADS_PER_GROUP = 4
GRID_W = 64
AXIAL_THETA = 10000.0
PARTIAL_THETA = 500000.0
PARTIAL_ROPE_DIM = HEAD_DIM // 4
EPS = 1e-6
MASK_VALUE = -1e30

LANES = 128
A_Q_W = A_Q_HEADS * HEAD_DIM
A_KV_W = A_KV_HEADS * HEAD_DIM
B_GROUP_W = B_HEADS_PER_GROUP * HEAD_DIM
B_W = len(B_DILATIONS) * B_GROUP_W
LOG2_E = math.log2(math.e)
Q_SCALE = HEAD_DIM ** -0.5 * LOG2_E

PROJ_CHUNK = 512
TILES_PER_STAGE = 4
VMEM_LIMIT = 56 * 1024 * 1024


def _const_spec(shape):
    return pl.BlockSpec(shape, lambda *_: (0,) * len(shape), pipeline_mode=pl.Buffered(1))


def _rope_tables(seq):
    t = np.arange(seq)

    def half_tables(pos, d, theta):
        d2 = d // 2
        freqs = theta ** (-(np.arange(d2, dtype=np.float64) * 2.0) / d)
        ang = pos.astype(np.float64)[:, None] * freqs[None, :]
        cos, sin = np.cos(ang), np.sin(ang)
        zero = np.zeros_like(sin)
        return (np.concatenate([cos, cos], -1), np.concatenate([-sin, zero], -1),
                np.concatenate([zero, sin], -1))

    half = HEAD_DIM // 2
    row = half_tables(t // GRID_W, half, AXIAL_THETA)
    col = half_tables(t % GRID_W, half, AXIAL_THETA)
    axial = [np.concatenate([r, c], -1) for r, c in zip(row, col)]
    part = half_tables(t, PARTIAL_ROPE_DIM, PARTIAL_THETA)
    rest = HEAD_DIM - PARTIAL_ROPE_DIM
    fill = (np.ones((seq, rest)), np.zeros((seq, rest)), np.zeros((seq, rest)))
    partial = [np.concatenate([p, f], -1) for p, f in zip(part, fill)]
    reps = LANES // HEAD_DIM
    tab_a = np.stack([np.tile(a, (1, reps)) for a in axial])
    tab_b = np.stack([np.tile(p, (1, reps)) for p in partial])
    return tab_a, tab_b


def _inproj_kernel(x_ref, gmix_ref, w_ref, qg_ref, kg_ref, seg_ref, ta_ref, tbq_ref, tbk_ref,
                   qa_ref, ka_ref, va_ref, b0_ref, b1_ref, b2_ref, gate_ref, z_ref, dl_ref, *, tn):
    x = x_ref[0]
    ms = jnp.mean(x * x, axis=-1, keepdims=True)
    h = (x * lax.rsqrt(ms + EPS) * gmix_ref[...]).astype(BF16)
    low = lax.broadcasted_iota(jnp.int32, (tn, LANES), 1) < HEAD_DIM
    n_slab = PROJ_CHUNK // LANES
    n_chunk = w_ref.shape[1] // PROJ_CHUNK
    first_gate = (A_Q_W + 2 * A_KV_W + 3 * B_W) // PROJ_CHUNK
    order = list(range(first_gate, n_chunk)) + list(range(first_gate))
    slot_of = {c: pos % 2 for pos, c in enumerate(order)}

    def stage(c):
        z = jnp.dot(h, w_ref[:, c * PROJ_CHUNK:(c + 1) * PROJ_CHUNK], preferred_element_type=F32)
        for s in range(n_slab):
            z_ref[slot_of[c], s] = z[:, s * LANES:(s + 1) * LANES]

    def slab(col):
        c, rem = divmod(col, PROJ_CHUNK)
        return z_ref.at[slot_of[c], rem // LANES]

    def inv_rms(zs, seg):
        z2 = jnp.concatenate([z * z for z in zs], axis=1).astype(BF16)
        ss = jnp.dot(z2, seg, preferred_element_type=F32)
        return lax.rsqrt(ss * (1.0 / HEAD_DIM) + EPS)

    def rope(z, tab_ref, sh):
        return (z * tab_ref[0] + pltpu.roll(z, LANES - sh, 1) * tab_ref[1]
                + pltpu.roll(z, sh, 1) * tab_ref[2])

    a_sh = HEAD_DIM // 4
    b_sh = PARTIAL_ROPE_DIM // 2
    kv_col = A_Q_W
    b_col = A_Q_W + 2 * A_KV_W
    g_col = b_col + 3 * B_W
    b_refs = (b0_ref, b1_ref, b2_ref)
    dl_slot = [0]

    def epilogue_a_q(col):
        zs = [slab(col)[...], slab(col + LANES)[...]]
        inv = inv_rms(zs, seg_ref[...])
        for i, z in enumerate(zs):
            zn = z * inv[:, i * LANES:(i + 1) * LANES] * qg_ref[...]
            c0 = col + i * LANES
            qa_ref[0, :, c0:c0 + LANES] = rope(zn, ta_ref, a_sh).astype(BF16)

    def epilogue_a_kv(col):
        zk = slab(col)[...]
        zk = rope(zk * inv_rms([zk], seg_ref[:LANES, :LANES]) * kg_ref[...], ta_ref, a_sh)
        zkr = pltpu.roll(zk, HEAD_DIM, 1)
        ka_ref[0, 0] = jnp.where(low, zk, zkr).astype(BF16)
        ka_ref[0, 1] = jnp.where(low, zkr, zk).astype(BF16)
        zv = slab(col + LANES)[...]
        va_ref[0, 0] = jnp.where(low, zv, 1.0).astype(BF16)
        va_ref[0, 1] = jnp.where(low, pltpu.roll(zv, HEAD_DIM, 1), 1.0).astype(BF16)

    def epilogue_b(col):
        t, rem = divmod(col - b_col, B_W)
        g, rem = divmod(rem, B_GROUP_W)
        d = B_DILATIONS[g]
        out_col = t * B_GROUP_W + rem
        tab_ref = (tbq_ref, tbk_ref, None)[t]
        src = slab(col)
        if tab_ref is not None:
            z = rope(src[...], tab_ref, b_sh)
            if d == 1:
                b_refs[g][0, 0, :, out_col:out_col + LANES] = z.astype(BF16)
                return
            src = dl_ref.at[dl_slot[0]]
            dl_slot[0] = 1 - dl_slot[0]
            src[...] = z
        elif d == 1:
            b_refs[g][0, 0, :, out_col:out_col + LANES] = src[...].astype(BF16)
            return
        for r in range(d):
            piece = src[pl.ds(r, tn // d, stride=d), :]
            b_refs[g][0, r, :, out_col:out_col + LANES] = piece.astype(BF16)

    def epilogue(c):
        col = c * PROJ_CHUNK
        while col < (c + 1) * PROJ_CHUNK:
            if col < kv_col:
                epilogue_a_q(col)
                col += 2 * LANES
            elif col < b_col:
                epilogue_a_kv(col)
                col += 2 * LANES
            elif col < g_col:
                epilogue_b(col)
                col += LANES
            else:
                gate = 1.0 / (1.0 + jnp.exp(-slab(col)[...]))
                gate_ref[0, :, col - g_col:col - g_col + LANES] = gate.astype(BF16)
                col += LANES

    stage(order[0])
    for pos, c in enumerate(order):
        if pos + 1 < n_chunk:
            stage(order[pos + 1])
        epilogue(c)


def _inproj(x, w_in, g_mix, q_gain, k_gain, tab_a, tab_bq, tab_bk, *, tn):
    bn, seq, dm = x.shape
    in_w = w_in.shape[1]
    gate_w = in_w - (A_Q_W + 2 * A_KV_W + 3 * B_W)
    grid = (bn, seq // tn)
    tab_spec = pl.BlockSpec((3, tn, LANES), lambda b, i: (0, i, 0))
    head_of = np.arange(2 * LANES) // HEAD_DIM
    seg = jnp.asarray(head_of[:, None] == head_of[None, :], BF16)
    out_shape = (
        jax.ShapeDtypeStruct((bn, seq, A_Q_W), BF16),
        jax.ShapeDtypeStruct((bn, A_KV_HEADS, seq, LANES), BF16),
        jax.ShapeDtypeStruct((bn, A_KV_HEADS, seq, LANES), BF16),
    ) + tuple(jax.ShapeDtypeStruct((bn, d, seq // d, 3 * B_GROUP_W), BF16) for d in B_DILATIONS) + (
        jax.ShapeDtypeStruct((bn, seq, gate_w), BF16),
    )
    out_specs = (
        pl.BlockSpec((1, tn, A_Q_W), lambda b, i: (b, i, 0)),
        pl.BlockSpec((1, A_KV_HEADS, tn, LANES), lambda b, i: (b, 0, i, 0)),
        pl.BlockSpec((1, A_KV_HEADS, tn, LANES), lambda b, i: (b, 0, i, 0)),
    ) + tuple(pl.BlockSpec((1, d, tn // d, 3 * B_GROUP_W), lambda b, i: (b, 0, i, 0))
              for d in B_DILATIONS) + (
        pl.BlockSpec((1, tn, gate_w), lambda b, i: (b, i, 0)),
    )
    return pl.pallas_call(
        functools.partial(_inproj_kernel, tn=tn),
        grid=grid,
        in_specs=[
            pl.BlockSpec((1, tn, dm), lambda b, i: (b, i, 0)),
            _const_spec((1, dm)),
            _const_spec((dm, in_w)),
            _const_spec((1, LANES)),
            _const_spec((1, LANES)),
            _const_spec(seg.shape),
            tab_spec, tab_spec, tab_spec,
        ],
        out_specs=out_specs,
        out_shape=out_shape,
        scratch_shapes=[pltpu.VMEM((2, PROJ_CHUNK // LANES, tn, LANES), F32),
                        pltpu.VMEM((2, tn, LANES), F32)],
        compiler_params=pltpu.CompilerParams(
            dimension_semantics=("parallel", "parallel"), vmem_limit_bytes=VMEM_LIMIT),
        name="inproj",
    )(x, g_mix, w_in, q_gain, k_gain, seg, tab_a, tab_bq, tab_bk)


def _attn_a_kernel(q_ref, k_ref, v_ref, o_ref, qs_ref, s_ref, pm_ref, m_ref, acc_ref,
                   *, tq, tk, nk):
    group = A_Q_HEADS // A_KV_HEADS
    n_lc = tk // LANES
    lane = lax.broadcasted_iota(jnp.int32, (tq, LANES), 1)
    low = lane < HEAD_DIM
    for h in range(group):
        pair = q_ref[0, :, (h // 2) * LANES:(h // 2 + 1) * LANES].astype(F32)
        keep = low if h % 2 == 0 else jnp.logical_not(low)
        qs_ref[h * tq:(h + 1) * tq, :] = jnp.where(keep, pair, 0.0).astype(BF16)
    m_ref[...] = jnp.full(m_ref.shape, MASK_VALUE, F32)
    acc_ref[...] = jnp.zeros(acc_ref.shape, F32)

    def scores(c):
        k = k_ref[0, 0, c * tk:(c + 1) * tk, :]
        s = lax.dot_general(qs_ref[...], k, (((1,), (1,)), ((), ())),
                            preferred_element_type=F32)
        s_ref[c % 2] = s
        pm = s[:, :LANES]
        for j in range(1, n_lc):
            pm = jnp.maximum(pm, s[:, j * LANES:(j + 1) * LANES])
        pm_ref[c % 2] = pm

    scores(0)
    for c in range(nk):
        if c + 1 < nk:
            scores(c + 1)
        slot = c % 2
        m_prev = m_ref[...]
        m_next = jnp.maximum(m_prev, jnp.max(pm_ref[slot], axis=1, keepdims=True))
        alpha = jnp.exp2(m_prev - m_next)
        m_ref[...] = m_next
        p = jnp.concatenate(
            [jnp.exp2(s_ref[slot, :, j * LANES:(j + 1) * LANES] - m_next) for j in range(n_lc)],
            axis=1).astype(BF16)
        v = v_ref[0, 0, c * tk:(c + 1) * tk, :]
        acc_ref[...] = alpha * acc_ref[...] + jnp.dot(p, v, preferred_element_type=F32)

    acc = acc_ref[...]
    acc_sw = pltpu.roll(acc, HEAD_DIM, 1)
    for pr in range(group // 2):
        ev = slice((2 * pr) * tq, (2 * pr + 1) * tq)
        od = slice((2 * pr + 1) * tq, (2 * pr + 2) * tq)
        out = jnp.where(low, acc[ev] / acc_sw[ev], acc_sw[od] / acc[od])
        o_ref[0, :, pr * LANES:(pr + 1) * LANES] = out.astype(BF16)


def _attn_a(qa, ka, va, *, tq, tk):
    bn, seq, _ = qa.shape
    group = A_Q_HEADS // A_KV_HEADS
    gw = group * HEAD_DIM
    grid = (bn, A_KV_HEADS, seq // tq)
    kv_spec = pl.BlockSpec((1, 1, seq, LANES), lambda b, j, i: (b, j, 0, 0))
    return pl.pallas_call(
        functools.partial(_attn_a_kernel, tq=tq, tk=tk, nk=seq // tk),
        grid=grid,
        in_specs=[pl.BlockSpec((1, tq, gw), lambda b, j, i: (b, i, j)), kv_spec, kv_spec],
        out_specs=pl.BlockSpec((1, tq, gw), lambda b, j, i: (b, i, j)),
        out_shape=jax.ShapeDtypeStruct((bn, seq, A_Q_W), BF16),
        scratch_shapes=[
            pltpu.VMEM((group * tq, LANES), BF16),
            pltpu.VMEM((2, group * tq, tk), F32),
            pltpu.VMEM((2, group * tq, LANES), F32),
            pltpu.VMEM((group * tq, LANES), F32),
            pltpu.VMEM((group * tq, LANES), F32),
        ],
        compiler_params=pltpu.CompilerParams(
            dimension_semantics=("parallel", "parallel", "parallel"), vmem_limit_bytes=VMEM_LIMIT),
        name="attn_a",
    )(qa, ka, va)


def _attn_b_kernel(q_ref, k_ref, v_ref, hmask_ref, bias_ref, o_ref, lse_ref,
                   s_scr, m_scr, *, sub_len, tq, tk, rb):
    nh = B_HEADS_PER_GROUP
    per_res = sub_len // tq
    n_tiles = rb * per_res
    low = lax.broadcasted_iota(jnp.int32, (tq, LANES), 1) < HEAD_DIM

    def locate(t):
        if isinstance(t, int):
            r, i = divmod(t, per_res)
            return r, i * tq, max(0, min(i * tq - B_HALF_WINDOW, sub_len - tk))
        r = t // per_res if rb > 1 else 0
        i0 = pl.multiple_of((t - r * per_res) * tq, tq)
        w0 = jnp.clip(i0 - B_HALF_WINDOW, 0, sub_len - tk)
        return r, i0, pl.multiple_of(w0, B_HALF_WINDOW)

    def lanes_to(x, width):
        return jnp.concatenate([x] * (width // LANES), axis=1)

    def stage_scores(t, slot):
        r, i0, w0 = locate(t)
        q = q_ref[0, r, pl.ds(i0, tq), :]
        k = k_ref[0, r, pl.ds(w0, tk), :]
        qs = jnp.concatenate([q * hmask_ref[h] for h in range(nh)], axis=0)
        s = lax.dot_general(qs, k, (((1,), (1,)), ((), ())), preferred_element_type=F32)
        bias = bias_ref[(i0 - w0) // B_HALF_WINDOW]
        s = s + jnp.concatenate([bias] * nh, axis=0)
        s_scr[slot] = s
        m_scr[slot] = jnp.broadcast_to(jnp.max(s, axis=1, keepdims=True), (nh * tq, LANES))

    def stage_finish(t, slot):
        r, i0, w0 = locate(t)
        v = v_ref[0, r, pl.ds(w0, tk), :]
        m = m_scr[slot]
        p = jnp.exp2(s_scr[slot] - lanes_to(m, tk))
        l = jnp.broadcast_to(jnp.sum(p, axis=1, keepdims=True), (nh * tq, LANES))
        lse = m + jnp.log(l) * LOG2_E
        pv = jnp.dot(p.astype(BF16), v, preferred_element_type=F32)
        for half in range(B_GROUP_W // LANES):
            lanes = slice(half * LANES, (half + 1) * LANES)
            ev = slice((2 * half) * tq, (2 * half + 1) * tq)
            od = slice((2 * half + 1) * tq, (2 * half + 2) * tq)
            num = jnp.where(low, pv[ev, lanes], pv[od, lanes])
            den = jnp.where(low, l[ev], l[od])
            o_ref[0, r, pl.ds(i0, tq), lanes] = (num / den).astype(BF16)
            lse_ref[0, r, pl.ds(i0, tq), lanes] = jnp.where(low, lse[ev], lse[od])

    gt = TILES_PER_STAGE
    n_groups = n_tiles // gt
    for u in range(gt):
        stage_scores(u, u)

    def step(g, carry):
        for u in range(gt):
            stage_finish(g * gt + u, u)
        for u in range(gt):
            stage_scores((g + 1) * gt + u, u)
        return carry

    lax.fori_loop(0, n_groups - 1, step, 0)
    for u in range(gt):
        stage_finish((n_groups - 1) * gt + u, u)


def _attn_b(qkv, *, tq=128, min_tiles=16):
    bn, d, sub_len, _ = qkv.shape
    tq = min(tq, sub_len)
    tk = min(tq + 2 * B_HALF_WINDOW, sub_len)
    per_res = sub_len // tq
    rb = min(d, max(1, min_tiles // per_res))
    assert (rb * per_res) % TILES_PER_STAGE == 0
    rows = B_HEADS_PER_GROUP * tq
    lane_head = np.arange(B_GROUP_W) // HEAD_DIM
    hmask = lane_head[None, None, :] == np.arange(B_HEADS_PER_GROUP)[:, None, None]
    hmask = jnp.asarray(np.broadcast_to(hmask, (B_HEADS_PER_GROUP, tq, B_GROUP_W)), BF16)
    rel = np.arange(tq)[:, None] - np.arange(tk)[None, :]
    offs = np.arange(3)[:, None, None] * B_HALF_WINDOW
    bias = jnp.asarray(np.where(np.abs(rel[None] + offs) <= B_HALF_WINDOW, 0.0, MASK_VALUE), F32)

    def spec(c):
        return pl.BlockSpec((1, rb, sub_len, B_GROUP_W), lambda b, r: (b, r, 0, c))

    out_spec = pl.BlockSpec((1, rb, sub_len, B_GROUP_W), lambda b, r: (b, r, 0, 0))
    return pl.pallas_call(
        functools.partial(_attn_b_kernel, sub_len=sub_len, tq=tq, tk=tk, rb=rb),
        grid=(bn, d // rb),
        in_specs=[spec(0), spec(1), spec(2), _const_spec(hmask.shape), _const_spec(bias.shape)],
        out_specs=(out_spec, out_spec),
        out_shape=(jax.ShapeDtypeStruct((bn, d, sub_len, B_GROUP_W), BF16),
                   jax.ShapeDtypeStruct((bn, d, sub_len, B_GROUP_W), F32)),
        scratch_shapes=[
            pltpu.VMEM((TILES_PER_STAGE, rows, tk), F32),
            pltpu.VMEM((TILES_PER_STAGE, rows, LANES), F32),
        ],
        compiler_params=pltpu.CompilerParams(
            dimension_semantics=("parallel", "parallel"), vmem_limit_bytes=VMEM_LIMIT),
        name="attn_b",
    )(qkv, qkv, qkv, hmask, bias)


def _post_kernel(x_ref, ya_ref, o0_ref, l0_ref, o1_ref, l1_ref, o2_ref, l2_ref, gate_ref,
                 wa_ref, wb_ref, wo_ref, gmlp_ref, wup_ref, wdn_ref, gfin_ref,
                 y_ref, ril_ref, *, tn, n_sub, ff_chunk, final_norm):
    dm = x_ref.shape[-1]
    n_slab = B_GROUP_W // LANES

    def to_natural_order(src_ref, d, slot):
        for r in range(d):
            blk = src_ref[0, r].astype(F32)
            for s in range(n_slab):
                ril_ref[slot * n_slab + s, pl.ds(r, tn // d, stride=d), :] = blk[:, s * LANES:(s + 1) * LANES]

    to_natural_order(o1_ref, B_DILATIONS[1], 0)
    to_natural_order(l1_ref, B_DILATIONS[1], 1)
    to_natural_order(o2_ref, B_DILATIONS[2], 2)
    to_natural_order(l2_ref, B_DILATIONS[2], 3)

    def natural(slot, rows):
        return jnp.concatenate([ril_ref[slot * n_slab + s, rows] for s in range(n_slab)], axis=1)

    def chain(rows):
        o0 = o0_ref[0, 0, rows].astype(F32)
        l0 = l0_ref[0, 0, rows]
        o1, l1, o2, l2 = (natural(slot, rows) for slot in range(4))
        mx = jnp.maximum(jnp.maximum(l0, l1), l2)
        e0, e1, e2 = jnp.exp2(l0 - mx), jnp.exp2(l1 - mx), jnp.exp2(l2 - mx)
        yb_in = (e0 * o0 + e1 * o1 + e2 * o2) / (e0 + e1 + e2)

        ya = jnp.dot(ya_ref[0, rows], wa_ref[...], preferred_element_type=F32)
        yb = jnp.dot(yb_in.astype(BF16), wb_ref[...], preferred_element_type=F32)
        ga = gate_ref[0, rows, :dm].astype(F32)
        gb = gate_ref[0, rows, dm:].astype(F32)
        mixed = ya * ga + yb * gb
        x1 = x_ref[0, rows] + jnp.dot(mixed.astype(BF16), wo_ref[...], preferred_element_type=F32)

        ms = jnp.mean(x1 * x1, axis=-1, keepdims=True)
        h2 = (x1 * lax.rsqrt(ms + EPS) * gmlp_ref[...]).astype(BF16)
        acc = x1
        d_ff = wup_ref.shape[1]
        for c in range(d_ff // ff_chunk):
            u = jnp.dot(h2, wup_ref[:, c * ff_chunk:(c + 1) * ff_chunk], preferred_element_type=F32)
            u = jnp.square(jnp.maximum(u, 0.0)).astype(BF16)
            acc = acc + jnp.dot(u, wdn_ref[c * ff_chunk:(c + 1) * ff_chunk, :],
                                preferred_element_type=F32)
        if final_norm:
            ms = jnp.mean(acc * acc, axis=-1, keepdims=True)
            acc = acc * lax.rsqrt(ms + EPS) * gfin_ref[...]
        y_ref[0, rows] = acc

    sub = tn // n_sub
    for i in range(n_sub):
        chain(slice(i * sub, (i + 1) * sub))


def _post(x, ya, b_outs, gates, w_a_out, w_b_out, w_out, g_mlp, w_up, w_down, g_final,
          *, tn, final_norm, n_sub=1, ff_chunk=512):
    bn, seq, dm = x.shape
    d_ff = w_up.shape[1]
    grid = (bn, seq // tn)
    in_specs = [
        pl.BlockSpec((1, tn, dm), lambda b, i: (b, i, 0)),
        pl.BlockSpec((1, tn, A_Q_W), lambda b, i: (b, i, 0)),
    ]
    operands = [x, ya]
    for d, (o, lse) in zip(B_DILATIONS, b_outs):
        spec = pl.BlockSpec((1, d, tn // d, B_GROUP_W), lambda b, i: (b, 0, i, 0))
        in_specs += [spec, spec]
        operands += [o, lse]
    in_specs += [
        pl.BlockSpec((1, tn, gates.shape[-1]), lambda b, i: (b, i, 0)),
        _const_spec(w_a_out.shape), _const_spec(w_b_out.shape), _const_spec(w_out.shape),
        _const_spec((1, dm)), _const_spec(w_up.shape), _const_spec(w_down.shape), _const_spec((1, dm)),
    ]
    operands += [gates, w_a_out, w_b_out, w_out, g_mlp, w_up, w_down, g_final]
    n_ril = 4 * (B_GROUP_W // LANES)
    return pl.pallas_call(
        functools.partial(_post_kernel, tn=tn, n_sub=n_sub, ff_chunk=ff_chunk,
                          final_norm=final_norm),
        grid=grid,
        in_specs=in_specs,
        out_specs=pl.BlockSpec((1, tn, dm), lambda b, i: (b, i, 0)),
        out_shape=jax.ShapeDtypeStruct((bn, seq, dm), F32),
        scratch_shapes=[pltpu.VMEM((n_ril, tn, LANES), F32)],
        compiler_params=pltpu.CompilerParams(
            dimension_semantics=("parallel", "parallel"), vmem_limit_bytes=VMEM_LIMIT),
        name="post",
    )(*operands)


def _layer(x, w, tabs, *, final_norm, g_final):
    tab_a, tab_bq, tab_bk = tabs
    qa, ka, va, b0, b1, b2, gates = _inproj(
        x, w["w_in"], w["g_mix"], w["q_gain"], w["k_gain"], tab_a, tab_bq, tab_bk, tn=512)
    ya = _attn_a(qa, ka, va, tq=512, tk=1024)
    b_outs = [_attn_b(b) for b in (b0, b1, b2)]
    return _post(x, ya, b_outs, gates, w["w_a_out"], w["w_b_out"], w["w_out"], w["g_mlp"],
                 w["w_up"], w["w_down"], g_final, tn=512, final_norm=final_norm)


def _trunk(x, layers, g_final):
    seq = x.shape[1]
    tab_a, tab_b = _rope_tables(seq)
    tabs = tuple(jnp.asarray(t, F32) for t in (tab_a, tab_b * Q_SCALE, tab_b))
    for i, w in enumerate(layers):
        x = _layer(x, w, tabs, final_norm=(i == len(layers) - 1), g_final=g_final)
    return x


def kernel(x_prompt, x_sample, w_in, w_a_out, w_b_out, w_out, g_mix, q_gain, k_gain, g_mlp,
           w_up, w_down, g_final):
    depth = w_in.shape[0]
    reps = LANES // HEAD_DIM
    layers = []
    for l in range(depth):
        layers.append(dict(
            w_in=w_in[l].astype(BF16), w_a_out=w_a_out[l].astype(BF16),
            w_b_out=w_b_out[l].astype(BF16), w_out=w_out[l].astype(BF16),
            w_up=w_up[l].astype(BF16), w_down=w_down[l].astype(BF16),
            g_mix=g_mix[l][None, :], g_mlp=g_mlp[l][None, :],
            q_gain=jnp.tile(q_gain[l] * Q_SCALE, reps)[None, :],
            k_gain=jnp.tile(k_gain[l], reps)[None, :],
        ))
    gf = g_final[None, :]
    return (_trunk(x_prompt, layers, gf), _trunk(x_sample, layers, gf))
```

```python
import functools
import math

import jax
import jax.numpy as jnp
import numpy as np
from jax import lax
from jax.experimental import pallas as pl
from jax.experimental.pallas import tpu as pltpu

F32 = jnp.float32
BF16 = jnp.bfloat16

HEAD_DIM = 64
A_Q_HEADS = 8
A_KV_HEADS = 2
B_DILATIONS = (1, 4, 16)
B_HALF_WINDOW = 64
B_HEADS_PER_GROUP = 4
GRID_W = 64
AXIAL_THETA = 10000.0
PARTIAL_THETA = 500000.0
PARTIAL_ROPE_DIM = HEAD_DIM // 4
EPS = 1e-6
MASK_VALUE = -1e30

LANES = 128
BF16_ROWS = 16
A_Q_W = A_Q_HEADS * HEAD_DIM
A_KV_W = A_KV_HEADS * HEAD_DIM
B_GROUP_W = B_HEADS_PER_GROUP * HEAD_DIM
B_W = len(B_DILATIONS) * B_GROUP_W
LOG2_E = math.log2(math.e)
Q_SCALE = HEAD_DIM ** -0.5 * LOG2_E

TOKEN_TILE = 512
PROJ_CHUNK = 512
MLP_CHUNK = 512
A_Q_TILE = 512
A_K_TILE = 1024
B_Q_TILE = 128
B_MIN_TILES = 16
TILES_PER_STAGE = 4
VMEM_LIMIT = 56 * 1024 * 1024


def _const_spec(shape):
    return pl.BlockSpec(shape, lambda *_: (0,) * len(shape), pipeline_mode=pl.Buffered(1))


def _rope_tables(seq):
    t = np.arange(seq)

    def half_tables(pos, d, theta):
        d2 = d // 2
        freqs = theta ** (-(np.arange(d2, dtype=np.float64) * 2.0) / d)
        ang = pos.astype(np.float64)[:, None] * freqs[None, :]
        cos, sin = np.cos(ang), np.sin(ang)
        zero = np.zeros_like(sin)
        return (np.concatenate([cos, cos], -1), np.concatenate([-sin, zero], -1),
                np.concatenate([zero, sin], -1))

    half = HEAD_DIM // 2
    row = half_tables(t // GRID_W, half, AXIAL_THETA)
    col = half_tables(t % GRID_W, half, AXIAL_THETA)
    axial = [np.concatenate([r, c], -1) for r, c in zip(row, col)]
    part = half_tables(t, PARTIAL_ROPE_DIM, PARTIAL_THETA)
    rest = HEAD_DIM - PARTIAL_ROPE_DIM
    fill = (np.ones((seq, rest)), np.zeros((seq, rest)), np.zeros((seq, rest)))
    partial = [np.concatenate([p, f], -1) for p, f in zip(part, fill)]
    reps = LANES // HEAD_DIM
    tab_a = np.stack([np.tile(a, (1, reps)) for a in axial])
    tab_b = np.stack([np.tile(p, (1, reps)) for p in partial])
    return tab_a, tab_b


def _inproj_kernel(x_ref, gmix_ref, w_ref, qg_ref, kg_ref, seg_ref, ta_ref, tbq_ref, tbk_ref,
                   qa_ref, ka_ref, va_ref, b0_ref, b1_ref, b2_ref, gate_ref, z_ref, dl_ref, *, tn):
    x = x_ref[0]
    ms = jnp.mean(x * x, axis=-1, keepdims=True)
    h = (x * lax.rsqrt(ms + EPS) * gmix_ref[...]).astype(BF16)
    low = lax.broadcasted_iota(jnp.int32, (tn, LANES), 1) < HEAD_DIM
    n_slab = PROJ_CHUNK // LANES
    n_chunk = w_ref.shape[1] // PROJ_CHUNK
    first_gate = (A_Q_W + 2 * A_KV_W + 3 * B_W) // PROJ_CHUNK
    order = list(range(first_gate, n_chunk)) + list(range(first_gate))
    slot_of = {c: pos % 2 for pos, c in enumerate(order)}

    def stage(c):
        z = jnp.dot(h, w_ref[:, c * PROJ_CHUNK:(c + 1) * PROJ_CHUNK], preferred_element_type=F32)
        for s in range(n_slab):
            z_ref[slot_of[c], s] = z[:, s * LANES:(s + 1) * LANES]

    def slab(col):
        c, rem = divmod(col, PROJ_CHUNK)
        return z_ref.at[slot_of[c], rem // LANES]

    def inv_rms(zs, seg):
        z2 = jnp.concatenate([z * z for z in zs], axis=1).astype(BF16)
        ss = jnp.dot(z2, seg, preferred_element_type=F32)
        return lax.rsqrt(ss * (1.0 / HEAD_DIM) + EPS)

    def rope(z, tab_ref, sh):
        return (z * tab_ref[0] + pltpu.roll(z, LANES - sh, 1) * tab_ref[1]
                + pltpu.roll(z, sh, 1) * tab_ref[2])

    a_sh = HEAD_DIM // 4
    b_sh = PARTIAL_ROPE_DIM // 2
    kv_col = A_Q_W
    b_col = A_Q_W + 2 * A_KV_W
    g_col = b_col + 3 * B_W
    b_refs = (b0_ref, b1_ref, b2_ref)
    dl_slot = [0]

    def epilogue_a_q(col):
        zs = [slab(col)[...], slab(col + LANES)[...]]
        inv = inv_rms(zs, seg_ref[...])
        for i, z in enumerate(zs):
            zn = z * inv[:, i * LANES:(i + 1) * LANES] * qg_ref[...]
            c0 = col + i * LANES
            qa_ref[0, :, c0:c0 + LANES] = rope(zn, ta_ref, a_sh).astype(BF16)

    def epilogue_a_kv(col):
        zk = slab(col)[...]
        zk = rope(zk * inv_rms([zk], seg_ref[:LANES, :LANES]) * kg_ref[...], ta_ref, a_sh)
        zkr = pltpu.roll(zk, HEAD_DIM, 1)
        ka_ref[0, 0] = jnp.where(low, zk, zkr).astype(BF16)
        ka_ref[0, 1] = jnp.where(low, zkr, zk).astype(BF16)
        zv = slab(col + LANES)[...]
        va_ref[0, 0] = jnp.where(low, zv, 1.0).astype(BF16)
        va_ref[0, 1] = jnp.where(low, pltpu.roll(zv, HEAD_DIM, 1), 1.0).astype(BF16)

    def epilogue_b(col):
        t, rem = divmod(col - b_col, B_W)
        g, rem = divmod(rem, B_GROUP_W)
        d = B_DILATIONS[g]
        out_col = t * B_GROUP_W + rem
        tab_ref = (tbq_ref, tbk_ref, None)[t]
        src = slab(col)
        if tab_ref is not None:
            z = rope(src[...], tab_ref, b_sh)
            if d == 1:
                b_refs[g][0, 0, :, out_col:out_col + LANES] = z.astype(BF16)
                return
            src = dl_ref.at[dl_slot[0]]
            dl_slot[0] = 1 - dl_slot[0]
            src[...] = z
        elif d == 1:
            b_refs[g][0, 0, :, out_col:out_col + LANES] = src[...].astype(BF16)
            return
        for r in range(d):
            piece = src[pl.ds(r, tn // d, stride=d), :]
            b_refs[g][0, r, :, out_col:out_col + LANES] = piece.astype(BF16)

    def epilogue(c):
        col = c * PROJ_CHUNK
        while col < (c + 1) * PROJ_CHUNK:
            if col < kv_col:
                epilogue_a_q(col)
                col += 2 * LANES
            elif col < b_col:
                epilogue_a_kv(col)
                col += 2 * LANES
            elif col < g_col:
                epilogue_b(col)
                col += LANES
            else:
                gate = 1.0 / (1.0 + jnp.exp(-slab(col)[...]))
                gate_ref[0, :, col - g_col:col - g_col + LANES] = gate.astype(BF16)
                col += LANES

    stage(order[0])
    for pos, c in enumerate(order):
        if pos + 1 < n_chunk:
            stage(order[pos + 1])
        epilogue(c)


def _inproj(x, w_in, g_mix, q_gain, k_gain, tab_a, tab_bq, tab_bk, *, tn):
    bn, seq, dm = x.shape
    in_w = w_in.shape[1]
    gate_w = in_w - (A_Q_W + 2 * A_KV_W + 3 * B_W)
    grid = (bn, seq // tn)
    tab_spec = pl.BlockSpec((3, tn, LANES), lambda b, i: (0, i, 0))
    head_of = np.arange(2 * LANES) // HEAD_DIM
    seg = jnp.asarray(head_of[:, None] == head_of[None, :], BF16)
    out_shape = (
        jax.ShapeDtypeStruct((bn, seq, A_Q_W), BF16),
        jax.ShapeDtypeStruct((bn, A_KV_HEADS, seq, LANES), BF16),
        jax.ShapeDtypeStruct((bn, A_KV_HEADS, seq, LANES), BF16),
    ) + tuple(jax.ShapeDtypeStruct((bn, d, seq // d, 3 * B_GROUP_W), BF16) for d in B_DILATIONS) + (
        jax.ShapeDtypeStruct((bn, seq, gate_w), BF16),
    )
    out_specs = (
        pl.BlockSpec((1, tn, A_Q_W), lambda b, i: (b, i, 0)),
        pl.BlockSpec((1, A_KV_HEADS, tn, LANES), lambda b, i: (b, 0, i, 0)),
        pl.BlockSpec((1, A_KV_HEADS, tn, LANES), lambda b, i: (b, 0, i, 0)),
    ) + tuple(pl.BlockSpec((1, d, tn // d, 3 * B_GROUP_W), lambda b, i: (b, 0, i, 0))
              for d in B_DILATIONS) + (
        pl.BlockSpec((1, tn, gate_w), lambda b, i: (b, i, 0)),
    )
    return pl.pallas_call(
        functools.partial(_inproj_kernel, tn=tn),
        grid=grid,
        in_specs=[
            pl.BlockSpec((1, tn, dm), lambda b, i: (b, i, 0)),
            _const_spec((1, dm)),
            _const_spec((dm, in_w)),
            _const_spec((1, LANES)),
            _const_spec((1, LANES)),
            _const_spec(seg.shape),
            tab_spec, tab_spec, tab_spec,
        ],
        out_specs=out_specs,
        out_shape=out_shape,
        scratch_shapes=[pltpu.VMEM((2, PROJ_CHUNK // LANES, tn, LANES), F32),
                        pltpu.VMEM((2, tn, LANES), F32)],
        compiler_params=pltpu.CompilerParams(
            dimension_semantics=("parallel", "parallel"), vmem_limit_bytes=VMEM_LIMIT),
        name="inproj",
    )(x, g_mix, w_in, q_gain, k_gain, seg, tab_a, tab_bq, tab_bk)


def _attn_a_kernel(q_ref, k_ref, v_ref, o_ref, qs_ref, s_ref, pm_ref, m_ref, acc_ref,
                   *, tq, tk, nk):
    group = A_Q_HEADS // A_KV_HEADS
    n_lc = tk // LANES
    lane = lax.broadcasted_iota(jnp.int32, (tq, LANES), 1)
    low = lane < HEAD_DIM
    for h in range(group):
        pair = q_ref[0, :, (h // 2) * LANES:(h // 2 + 1) * LANES].astype(F32)
        keep = low if h % 2 == 0 else jnp.logical_not(low)
        qs_ref[h * tq:(h + 1) * tq, :] = jnp.where(keep, pair, 0.0).astype(BF16)
    m_ref[...] = jnp.full(m_ref.shape, MASK_VALUE, F32)
    acc_ref[...] = jnp.zeros(acc_ref.shape, F32)

    def scores(c):
        k = k_ref[0, 0, c * tk:(c + 1) * tk, :]
        s = lax.dot_general(qs_ref[...], k, (((1,), (1,)), ((), ())),
                            preferred_element_type=F32)
        s_ref[c % 2] = s
        pm = s[:, :LANES]
        for j in range(1, n_lc):
            pm = jnp.maximum(pm, s[:, j * LANES:(j + 1) * LANES])
        pm_ref[c % 2] = pm

    scores(0)
    for c in range(nk):
        if c + 1 < nk:
            scores(c + 1)
        slot = c % 2
        m_prev = m_ref[...]
        m_next = jnp.maximum(m_prev, jnp.max(pm_ref[slot], axis=1, keepdims=True))
        alpha = jnp.exp2(m_prev - m_next)
        m_ref[...] = m_next
        p = jnp.concatenate(
            [jnp.exp2(s_ref[slot, :, j * LANES:(j + 1) * LANES] - m_next) for j in range(n_lc)],
            axis=1).astype(BF16)
        v = v_ref[0, 0, c * tk:(c + 1) * tk, :]
        acc_ref[...] = alpha * acc_ref[...] + jnp.dot(p, v, preferred_element_type=F32)

    acc = acc_ref[...]
    acc_sw = pltpu.roll(acc, HEAD_DIM, 1)
    for pr in range(group // 2):
        ev = slice((2 * pr) * tq, (2 * pr + 1) * tq)
        od = slice((2 * pr + 1) * tq, (2 * pr + 2) * tq)
        out = jnp.where(low, acc[ev] / acc_sw[ev], acc_sw[od] / acc[od])
        o_ref[0, :, pr * LANES:(pr + 1) * LANES] = out.astype(BF16)


def _attn_a(qa, ka, va, *, tq, tk):
    bn, seq, _ = qa.shape
    group = A_Q_HEADS // A_KV_HEADS
    gw = group * HEAD_DIM
    grid = (bn, A_KV_HEADS, seq // tq)
    kv_spec = pl.BlockSpec((1, 1, seq, LANES), lambda b, j, i: (b, j, 0, 0))
    return pl.pallas_call(
        functools.partial(_attn_a_kernel, tq=tq, tk=tk, nk=seq // tk),
        grid=grid,
        in_specs=[pl.BlockSpec((1, tq, gw), lambda b, j, i: (b, i, j)), kv_spec, kv_spec],
        out_specs=pl.BlockSpec((1, tq, gw), lambda b, j, i: (b, i, j)),
        out_shape=jax.ShapeDtypeStruct((bn, seq, A_Q_W), BF16),
        scratch_shapes=[
            pltpu.VMEM((group * tq, LANES), BF16),
            pltpu.VMEM((2, group * tq, tk), F32),
            pltpu.VMEM((2, group * tq, LANES), F32),
            pltpu.VMEM((group * tq, LANES), F32),
            pltpu.VMEM((group * tq, LANES), F32),
        ],
        compiler_params=pltpu.CompilerParams(
            dimension_semantics=("parallel", "parallel", "parallel"), vmem_limit_bytes=VMEM_LIMIT),
        name="attn_a",
    )(qa, ka, va)


def _attn_b_kernel(q_ref, k_ref, v_ref, hmask_ref, bias_ref, o_ref, lse_ref,
                   s_scr, m_scr, *, sub_len, tq, tk, rb):
    nh = B_HEADS_PER_GROUP
    per_res = sub_len // tq
    n_tiles = rb * per_res
    low = lax.broadcasted_iota(jnp.int32, (tq, LANES), 1) < HEAD_DIM

    def locate(t):
        if isinstance(t, int):
            r, i = divmod(t, per_res)
            return r, i * tq, max(0, min(i * tq - B_HALF_WINDOW, sub_len - tk))
        r = t // per_res if rb > 1 else 0
        i0 = pl.multiple_of((t - r * per_res) * tq, tq)
        w0 = jnp.clip(i0 - B_HALF_WINDOW, 0, sub_len - tk)
        return r, i0, pl.multiple_of(w0, B_HALF_WINDOW)

    def lanes_to(x, width):
        return jnp.concatenate([x] * (width // LANES), axis=1)

    def stage_scores(t, slot):
        r, i0, w0 = locate(t)
        q = q_ref[0, r, pl.ds(i0, tq), :]
        k = k_ref[0, r, pl.ds(w0, tk), :]
        qs = jnp.concatenate([q * hmask_ref[h] for h in range(nh)], axis=0)
        s = lax.dot_general(qs, k, (((1,), (1,)), ((), ())), preferred_element_type=F32)
        bias = bias_ref[(i0 - w0) // B_HALF_WINDOW]
        s = s + jnp.concatenate([bias] * nh, axis=0)
        s_scr[slot] = s
        m_scr[slot] = jnp.broadcast_to(jnp.max(s, axis=1, keepdims=True), (nh * tq, LANES))

    def stage_finish(t, slot):
        r, i0, w0 = locate(t)
        v = v_ref[0, r, pl.ds(w0, tk), :]
        m = m_scr[slot]
        p = jnp.exp2(s_scr[slot] - lanes_to(m, tk))
        l = jnp.broadcast_to(jnp.sum(p, axis=1, keepdims=True), (nh * tq, LANES))
        lse = m + jnp.log(l) * LOG2_E
        pv = jnp.dot(p.astype(BF16), v, preferred_element_type=F32)
        for half in range(B_GROUP_W // LANES):
            lanes = slice(half * LANES, (half + 1) * LANES)
            ev = slice((2 * half) * tq, (2 * half + 1) * tq)
            od = slice((2 * half + 1) * tq, (2 * half + 2) * tq)
            num = jnp.where(low, pv[ev, lanes], pv[od, lanes])
            den = jnp.where(low, l[ev], l[od])
            o_ref[0, r, pl.ds(i0, tq), lanes] = (num / den).astype(BF16)
            lse_ref[0, r, pl.ds(i0, tq), lanes] = jnp.where(low, lse[ev], lse[od])

    gt = TILES_PER_STAGE
    n_groups = n_tiles // gt
    for u in range(gt):
        stage_scores(u, u)

    def step(g, carry):
        for u in range(gt):
            stage_finish(g * gt + u, u)
        for u in range(gt):
            stage_scores((g + 1) * gt + u, u)
        return carry

    lax.fori_loop(0, n_groups - 1, step, 0)
    for u in range(gt):
        stage_finish((n_groups - 1) * gt + u, u)


def _attn_b(qkv, *, tq=B_Q_TILE, min_tiles=B_MIN_TILES):
    bn, d, sub_len, _ = qkv.shape
    tq = min(tq, sub_len)
    tk = min(tq + 2 * B_HALF_WINDOW, sub_len)
    per_res = sub_len // tq
    rb = min(d, max(1, min_tiles // per_res))
    assert (rb * per_res) % TILES_PER_STAGE == 0
    rows = B_HEADS_PER_GROUP * tq
    lane_head = np.arange(B_GROUP_W) // HEAD_DIM
    hmask = lane_head[None, None, :] == np.arange(B_HEADS_PER_GROUP)[:, None, None]
    hmask = jnp.asarray(np.broadcast_to(hmask, (B_HEADS_PER_GROUP, tq, B_GROUP_W)), BF16)
    rel = np.arange(tq)[:, None] - np.arange(tk)[None, :]
    offs = np.arange(3)[:, None, None] * B_HALF_WINDOW
    bias = jnp.asarray(np.where(np.abs(rel[None] + offs) <= B_HALF_WINDOW, 0.0, MASK_VALUE), F32)

    def spec(c):
        return pl.BlockSpec((1, rb, sub_len, B_GROUP_W), lambda b, r: (b, r, 0, c))

    out_spec = pl.BlockSpec((1, rb, sub_len, B_GROUP_W), lambda b, r: (b, r, 0, 0))
    return pl.pallas_call(
        functools.partial(_attn_b_kernel, sub_len=sub_len, tq=tq, tk=tk, rb=rb),
        grid=(bn, d // rb),
        in_specs=[spec(0), spec(1), spec(2), _const_spec(hmask.shape), _const_spec(bias.shape)],
        out_specs=(out_spec, out_spec),
        out_shape=(jax.ShapeDtypeStruct((bn, d, sub_len, B_GROUP_W), BF16),
                   jax.ShapeDtypeStruct((bn, d, sub_len, B_GROUP_W), F32)),
        scratch_shapes=[
            pltpu.VMEM((TILES_PER_STAGE, rows, tk), F32),
            pltpu.VMEM((TILES_PER_STAGE, rows, LANES), F32),
        ],
        compiler_params=pltpu.CompilerParams(
            dimension_semantics=("parallel", "parallel"), vmem_limit_bytes=VMEM_LIMIT),
        name="attn_b",
    )(qkv, qkv, qkv, hmask, bias)


def _post_kernel(x_ref, ya_ref, o0_ref, l0_ref, o1_ref, l1_ref, o2_ref, l2_ref, gate_ref,
                 wa_ref, wb_ref, wo_ref, gmlp_ref, wup_ref, wdn_ref, gfin_ref,
                 y_ref, ril_ref, *, tn, ff_chunk, final_norm):
    dm = x_ref.shape[-1]
    n_slab = B_GROUP_W // LANES

    def to_natural_order(src_ref, d, slot):
        for r in range(d):
            blk = src_ref[0, r].astype(F32)
            for s in range(n_slab):
                ril_ref[slot * n_slab + s, pl.ds(r, tn // d, stride=d), :] = blk[:, s * LANES:(s + 1) * LANES]

    to_natural_order(o1_ref, B_DILATIONS[1], 0)
    to_natural_order(l1_ref, B_DILATIONS[1], 1)
    to_natural_order(o2_ref, B_DILATIONS[2], 2)
    to_natural_order(l2_ref, B_DILATIONS[2], 3)

    def natural(slot):
        return jnp.concatenate([ril_ref[slot * n_slab + s] for s in range(n_slab)], axis=1)

    o0 = o0_ref[0, 0].astype(F32)
    l0 = l0_ref[0, 0]
    o1, l1, o2, l2 = (natural(slot) for slot in range(4))
    mx = jnp.maximum(jnp.maximum(l0, l1), l2)
    e0, e1, e2 = jnp.exp2(l0 - mx), jnp.exp2(l1 - mx), jnp.exp2(l2 - mx)
    yb_in = (e0 * o0 + e1 * o1 + e2 * o2) / (e0 + e1 + e2)

    ya = jnp.dot(ya_ref[0], wa_ref[...], preferred_element_type=F32)
    yb = jnp.dot(yb_in.astype(BF16), wb_ref[...], preferred_element_type=F32)
    ga = gate_ref[0, :, :dm].astype(F32)
    gb = gate_ref[0, :, dm:].astype(F32)
    mixed = ya * ga + yb * gb
    x1 = x_ref[0] + jnp.dot(mixed.astype(BF16), wo_ref[...], preferred_element_type=F32)

    ms = jnp.mean(x1 * x1, axis=-1, keepdims=True)
    h2 = (x1 * lax.rsqrt(ms + EPS) * gmlp_ref[...]).astype(BF16)
    acc = x1
    d_ff = wup_ref.shape[1]
    for c in range(d_ff // ff_chunk):
        u = jnp.dot(h2, wup_ref[:, c * ff_chunk:(c + 1) * ff_chunk], preferred_element_type=F32)
        u = jnp.square(jnp.maximum(u, 0.0)).astype(BF16)
        acc = acc + jnp.dot(u, wdn_ref[c * ff_chunk:(c + 1) * ff_chunk, :],
                            preferred_element_type=F32)
    if final_norm:
        ms = jnp.mean(acc * acc, axis=-1, keepdims=True)
        acc = acc * lax.rsqrt(ms + EPS) * gfin_ref[...]
    y_ref[0] = acc


def _post(x, ya, b_outs, gates, w_a_out, w_b_out, w_out, g_mlp, w_up, w_down, g_final,
          *, tn, final_norm, ff_chunk=MLP_CHUNK):
    bn, seq, dm = x.shape
    d_ff = w_up.shape[1]
    grid = (bn, seq // tn)
    in_specs = [
        pl.BlockSpec((1, tn, dm), lambda b, i: (b, i, 0)),
        pl.BlockSpec((1, tn, A_Q_W), lambda b, i: (b, i, 0)),
    ]
    operands = [x, ya]
    for d, (o, lse) in zip(B_DILATIONS, b_outs):
        spec = pl.BlockSpec((1, d, tn // d, B_GROUP_W), lambda b, i: (b, 0, i, 0))
        in_specs += [spec, spec]
        operands += [o, lse]
    in_specs += [
        pl.BlockSpec((1, tn, gates.shape[-1]), lambda b, i: (b, i, 0)),
        _const_spec(w_a_out.shape), _const_spec(w_b_out.shape), _const_spec(w_out.shape),
        _const_spec((1, dm)), _const_spec(w_up.shape), _const_spec(w_down.shape), _const_spec((1, dm)),
    ]
    operands += [gates, w_a_out, w_b_out, w_out, g_mlp, w_up, w_down, g_final]
    n_ril = 4 * (B_GROUP_W // LANES)
    return pl.pallas_call(
        functools.partial(_post_kernel, tn=tn, ff_chunk=ff_chunk, final_norm=final_norm),
        grid=grid,
        in_specs=in_specs,
        out_specs=pl.BlockSpec((1, tn, dm), lambda b, i: (b, i, 0)),
        out_shape=jax.ShapeDtypeStruct((bn, seq, dm), F32),
        scratch_shapes=[pltpu.VMEM((n_ril, tn, LANES), F32)],
        compiler_params=pltpu.CompilerParams(
            dimension_semantics=("parallel", "parallel"), vmem_limit_bytes=VMEM_LIMIT),
        name="post",
    )(*operands)


def _layer(x, w, tabs, *, final_norm, g_final):
    tab_a, tab_bq, tab_bk = tabs
    seq = x.shape[1]
    assert seq % TOKEN_TILE == 0 and seq % A_Q_TILE == 0 and seq % A_K_TILE == 0
    assert TOKEN_TILE % (BF16_ROWS * max(B_DILATIONS)) == 0
    qa, ka, va, b0, b1, b2, gates = _inproj(
        x, w["w_in"], w["g_mix"], w["q_gain"], w["k_gain"], tab_a, tab_bq, tab_bk, tn=TOKEN_TILE)
    ya = _attn_a(qa, ka, va, tq=A_Q_TILE, tk=A_K_TILE)
    b_outs = [_attn_b(b) for b in (b0, b1, b2)]
    return _post(x, ya, b_outs, gates, w["w_a_out"], w["w_b_out"], w["w_out"], w["g_mlp"],
                 w["w_up"], w["w_down"], g_final, tn=TOKEN_TILE, final_norm=final_norm)


def _trunk(x, layers, g_final):
    seq = x.shape[1]
    tab_a, tab_b = _rope_tables(seq)
    tabs = tuple(jnp.asarray(t, F32) for t in (tab_a, tab_b * Q_SCALE, tab_b))
    for i, w in enumerate(layers):
        x = _layer(x, w, tabs, final_norm=(i == len(layers) - 1), g_final=g_final)
    return x


def kernel(x_prompt, x_sample, w_in, w_a_out, w_b_out, w_out, g_mix, q_gain, k_gain, g_mlp,
           w_up, w_down, g_final):
    depth = w_in.shape[0]
    reps = LANES // HEAD_DIM
    layers = []
    for l in range(depth):
        layers.append(dict(
            w_in=w_in[l].astype(BF16), w_a_out=w_a_out[l].astype(BF16),
            w_b_out=w_b_out[l].astype(BF16), w_out=w_out[l].astype(BF16),
            w_up=w_up[l].astype(BF16), w_down=w_down[l].astype(BF16),
            g_mix=g_mix[l][None, :], g_mlp=g_mlp[l][None, :],
            q_gain=jnp.tile(q_gain[l] * Q_SCALE, reps)[None, :],
            k_gain=jnp.tile(k_gain[l], reps)[None, :],
        ))
    gf = g_final[None, :]
    return (_trunk(x_prompt, layers, gf), _trunk(x_sample, layers, gf))
```

```python
import functools
import math

import jax
import jax.numpy as jnp
import numpy as np
from jax import lax
from jax.experimental import pallas as pl
from jax.experimental.pallas import tpu as pltpu

F32 = jnp.float32
BF16 = jnp.bfloat16

HEAD_DIM = 64
A_Q_HEADS = 8
A_KV_HEADS = 2
B_DILATIONS = (1, 4, 16)
B_HALF_WINDOW = 64
B_HEADS_PER_GROUP = 4
GRID_W = 64
AXIAL_THETA = 10000.0
PARTIAL_THETA = 500000.0
PARTIAL_ROPE_DIM = HEAD_DIM // 4
EPS = 1e-6
MASK_VALUE = -1e30

LANES = 128
BF16_ROWS = 16
A_Q_W = A_Q_HEADS * HEAD_DIM
A_KV_W = A_KV_HEADS * HEAD_DIM
B_GROUP_W = B_HEADS_PER_GROUP * HEAD_DIM
B_W = len(B_DILATIONS) * B_GROUP_W
LOG2_E = math.log2(math.e)
Q_SCALE = HEAD_DIM ** -0.5 * LOG2_E

TOKEN_TILE = 512
PROJ_CHUNK = 512
MLP_CHUNK = 512
A_Q_TILE = 1024
A_K_TILE = 1024
B_Q_TILE = 128
B_MIN_TILES = 16
TILES_PER_STAGE = 4
VMEM_LIMIT = 56 * 1024 * 1024


def _const_spec(shape):
    return pl.BlockSpec(shape, lambda *_: (0,) * len(shape), pipeline_mode=pl.Buffered(1))


def _rope_tables(seq):
    t = np.arange(seq)

    def half_tables(pos, d, theta):
        d2 = d // 2
        freqs = theta ** (-(np.arange(d2, dtype=np.float64) * 2.0) / d)
        ang = pos.astype(np.float64)[:, None] * freqs[None, :]
        cos, sin = np.cos(ang), np.sin(ang)
        zero = np.zeros_like(sin)
        return (np.concatenate([cos, cos], -1), np.concatenate([-sin, zero], -1),
                np.concatenate([zero, sin], -1))

    half = HEAD_DIM // 2
    row = half_tables(t // GRID_W, half, AXIAL_THETA)
    col = half_tables(t % GRID_W, half, AXIAL_THETA)
    axial = [np.concatenate([r, c], -1) for r, c in zip(row, col)]
    part = half_tables(t, PARTIAL_ROPE_DIM, PARTIAL_THETA)
    rest = HEAD_DIM - PARTIAL_ROPE_DIM
    fill = (np.ones((seq, rest)), np.zeros((seq, rest)), np.zeros((seq, rest)))
    partial = [np.concatenate([p, f], -1) for p, f in zip(part, fill)]
    reps = LANES // HEAD_DIM
    tab_a = np.stack([np.tile(a, (1, reps)) for a in axial])
    tab_b = np.stack([np.tile(p, (1, reps)) for p in partial])
    return tab_a, tab_b


def _inproj_kernel(x_ref, gmix_ref, w_ref, qg_ref, kg_ref, seg_ref, ta_ref, tbq_ref, tbk_ref,
                   qa_ref, ka_ref, va_ref, b0_ref, b1_ref, b2_ref, gate_ref, z_ref, dl_ref, *, tn):
    x = x_ref[0]
    ms = jnp.mean(x * x, axis=-1, keepdims=True)
    h = (x * lax.rsqrt(ms + EPS) * gmix_ref[...]).astype(BF16)
    low = lax.broadcasted_iota(jnp.int32, (tn, LANES), 1) < HEAD_DIM
    n_slab = PROJ_CHUNK // LANES
    n_chunk = w_ref.shape[1] // PROJ_CHUNK
    first_gate = (A_Q_W + 2 * A_KV_W + 3 * B_W) // PROJ_CHUNK
    order = list(range(first_gate, n_chunk)) + list(range(first_gate))
    slot_of = {c: pos % 2 for pos, c in enumerate(order)}

    def stage(c):
        z = jnp.dot(h, w_ref[:, c * PROJ_CHUNK:(c + 1) * PROJ_CHUNK], preferred_element_type=F32)
        for s in range(n_slab):
            z_ref[slot_of[c], s] = z[:, s * LANES:(s + 1) * LANES]

    def slab(col):
        c, rem = divmod(col, PROJ_CHUNK)
        return z_ref.at[slot_of[c], rem // LANES]

    def inv_rms(zs, seg):
        z2 = jnp.concatenate([z * z for z in zs], axis=1).astype(BF16)
        ss = jnp.dot(z2, seg, preferred_element_type=F32)
        return lax.rsqrt(ss * (1.0 / HEAD_DIM) + EPS)

    def rope(z, tab_ref, sh):
        return (z * tab_ref[0] + pltpu.roll(z, LANES - sh, 1) * tab_ref[1]
                + pltpu.roll(z, sh, 1) * tab_ref[2])

    a_sh = HEAD_DIM // 4
    b_sh = PARTIAL_ROPE_DIM // 2
    kv_col = A_Q_W
    b_col = A_Q_W + 2 * A_KV_W
    g_col = b_col + 3 * B_W
    b_refs = (b0_ref, b1_ref, b2_ref)
    dl_slot = [0]

    def epilogue_a_q(col):
        zs = [slab(col)[...], slab(col + LANES)[...]]
        inv = inv_rms(zs, seg_ref[...])
        for i, z in enumerate(zs):
            zn = z * inv[:, i * LANES:(i + 1) * LANES] * qg_ref[...]
            c0 = col + i * LANES
            qa_ref[0, :, c0:c0 + LANES] = rope(zn, ta_ref, a_sh).astype(BF16)

    def epilogue_a_kv(col):
        zk = slab(col)[...]
        zk = rope(zk * inv_rms([zk], seg_ref[:LANES, :LANES]) * kg_ref[...], ta_ref, a_sh)
        zkr = pltpu.roll(zk, HEAD_DIM, 1)
        ka_ref[0, 0] = jnp.where(low, zk, zkr).astype(BF16)
        ka_ref[0, 1] = jnp.where(low, zkr, zk).astype(BF16)
        zv = slab(col + LANES)[...]
        va_ref[0, 0] = jnp.where(low, zv, 1.0).astype(BF16)
        va_ref[0, 1] = jnp.where(low, pltpu.roll(zv, HEAD_DIM, 1), 1.0).astype(BF16)

    def epilogue_b(col):
        t, rem = divmod(col - b_col, B_W)
        g, rem = divmod(rem, B_GROUP_W)
        d = B_DILATIONS[g]
        out_col = t * B_GROUP_W + rem
        tab_ref = (tbq_ref, tbk_ref, None)[t]
        src = slab(col)
        if tab_ref is not None:
            z = rope(src[...], tab_ref, b_sh)
            if d == 1:
                b_refs[g][0, 0, :, out_col:out_col + LANES] = z.astype(BF16)
                return
            src = dl_ref.at[dl_slot[0]]
            dl_slot[0] = 1 - dl_slot[0]
            src[...] = z
        elif d == 1:
            b_refs[g][0, 0, :, out_col:out_col + LANES] = src[...].astype(BF16)
            return
        for r in range(d):
            piece = src[pl.ds(r, tn // d, stride=d), :]
            b_refs[g][0, r, :, out_col:out_col + LANES] = piece.astype(BF16)

    def epilogue(c):
        col = c * PROJ_CHUNK
        while col < (c + 1) * PROJ_CHUNK:
            if col < kv_col:
                epilogue_a_q(col)
                col += 2 * LANES
            elif col < b_col:
                epilogue_a_kv(col)
                col += 2 * LANES
            elif col < g_col:
                epilogue_b(col)
                col += LANES
            else:
                gate = 1.0 / (1.0 + jnp.exp(-slab(col)[...]))
                gate_ref[0, :, col - g_col:col - g_col + LANES] = gate.astype(BF16)
                col += LANES

    stage(order[0])
    for pos, c in enumerate(order):
        if pos + 1 < n_chunk:
            stage(order[pos + 1])
        epilogue(c)


def _inproj(x, w_in, g_mix, q_gain, k_gain, tab_a, tab_bq, tab_bk, *, tn):
    bn, seq, dm = x.shape
    in_w = w_in.shape[1]
    gate_w = in_w - (A_Q_W + 2 * A_KV_W + 3 * B_W)
    grid = (bn, seq // tn)
    tab_spec = pl.BlockSpec((3, tn, LANES), lambda b, i: (0, i, 0))
    head_of = np.arange(2 * LANES) // HEAD_DIM
    seg = jnp.asarray(head_of[:, None] == head_of[None, :], BF16)
    out_shape = (
        jax.ShapeDtypeStruct((bn, seq, A_Q_W), BF16),
        jax.ShapeDtypeStruct((bn, A_KV_HEADS, seq, LANES), BF16),
        jax.ShapeDtypeStruct((bn, A_KV_HEADS, seq, LANES), BF16),
    ) + tuple(jax.ShapeDtypeStruct((bn, d, seq // d, 3 * B_GROUP_W), BF16) for d in B_DILATIONS) + (
        jax.ShapeDtypeStruct((bn, seq, gate_w), BF16),
    )
    out_specs = (
        pl.BlockSpec((1, tn, A_Q_W), lambda b, i: (b, i, 0)),
        pl.BlockSpec((1, A_KV_HEADS, tn, LANES), lambda b, i: (b, 0, i, 0)),
        pl.BlockSpec((1, A_KV_HEADS, tn, LANES), lambda b, i: (b, 0, i, 0)),
    ) + tuple(pl.BlockSpec((1, d, tn // d, 3 * B_GROUP_W), lambda b, i: (b, 0, i, 0))
              for d in B_DILATIONS) + (
        pl.BlockSpec((1, tn, gate_w), lambda b, i: (b, i, 0)),
    )
    return pl.pallas_call(
        functools.partial(_inproj_kernel, tn=tn),
        grid=grid,
        in_specs=[
            pl.BlockSpec((1, tn, dm), lambda b, i: (b, i, 0)),
            _const_spec((1, dm)),
            _const_spec((dm, in_w)),
            _const_spec((1, LANES)),
            _const_spec((1, LANES)),
            _const_spec(seg.shape),
            tab_spec, tab_spec, tab_spec,
        ],
        out_specs=out_specs,
        out_shape=out_shape,
        scratch_shapes=[pltpu.VMEM((2, PROJ_CHUNK // LANES, tn, LANES), F32),
                        pltpu.VMEM((2, tn, LANES), F32)],
        compiler_params=pltpu.CompilerParams(
            dimension_semantics=("parallel", "parallel"), vmem_limit_bytes=VMEM_LIMIT),
        name="inproj",
    )(x, g_mix, w_in, q_gain, k_gain, seg, tab_a, tab_bq, tab_bk)


def _attn_a_kernel(q_ref, k_ref, v_ref, o_ref, qs_ref, s_ref, pm_ref, m_ref, acc_ref,
                   *, tq, tk, nk):
    group = A_Q_HEADS // A_KV_HEADS
    n_lc = tk // LANES
    lane = lax.broadcasted_iota(jnp.int32, (tq, LANES), 1)
    low = lane < HEAD_DIM
    for h in range(group):
        pair = q_ref[0, :, (h // 2) * LANES:(h // 2 + 1) * LANES].astype(F32)
        keep = low if h % 2 == 0 else jnp.logical_not(low)
        qs_ref[h * tq:(h + 1) * tq, :] = jnp.where(keep, pair, 0.0).astype(BF16)
    m_ref[...] = jnp.full(m_ref.shape, MASK_VALUE, F32)
    acc_ref[...] = jnp.zeros(acc_ref.shape, F32)

    def scores(c):
        k = k_ref[0, 0, c * tk:(c + 1) * tk, :]
        s = lax.dot_general(qs_ref[...], k, (((1,), (1,)), ((), ())),
                            preferred_element_type=F32)
        s_ref[c % 2] = s
        pm = s[:, :LANES]
        for j in range(1, n_lc):
            pm = jnp.maximum(pm, s[:, j * LANES:(j + 1) * LANES])
        pm_ref[c % 2] = pm

    scores(0)
    for c in range(nk):
        if c + 1 < nk:
            scores(c + 1)
        slot = c % 2
        m_prev = m_ref[...]
        m_next = jnp.maximum(m_prev, jnp.max(pm_ref[slot], axis=1, keepdims=True))
        alpha = jnp.exp2(m_prev - m_next)
        m_ref[...] = m_next
        p = jnp.concatenate(
            [jnp.exp2(s_ref[slot, :, j * LANES:(j + 1) * LANES] - m_next) for j in range(n_lc)],
            axis=1).astype(BF16)
        v = v_ref[0, 0, c * tk:(c + 1) * tk, :]
        acc_ref[...] = alpha * acc_ref[...] + jnp.dot(p, v, preferred_element_type=F32)

    acc = acc_ref[...]
    acc_sw = pltpu.roll(acc, HEAD_DIM, 1)
    for pr in range(group // 2):
        ev = slice((2 * pr) * tq, (2 * pr + 1) * tq)
        od = slice((2 * pr + 1) * tq, (2 * pr + 2) * tq)
        out = jnp.where(low, acc[ev] / acc_sw[ev], acc_sw[od] / acc[od])
        o_ref[0, :, pr * LANES:(pr + 1) * LANES] = out.astype(BF16)


def _attn_a(qa, ka, va, *, tq, tk):
    bn, seq, _ = qa.shape
    group = A_Q_HEADS // A_KV_HEADS
    gw = group * HEAD_DIM
    grid = (bn, A_KV_HEADS, seq // tq)
    kv_spec = pl.BlockSpec((1, 1, seq, LANES), lambda b, j, i: (b, j, 0, 0))
    return pl.pallas_call(
        functools.partial(_attn_a_kernel, tq=tq, tk=tk, nk=seq // tk),
        grid=grid,
        in_specs=[pl.BlockSpec((1, tq, gw), lambda b, j, i: (b, i, j)), kv_spec, kv_spec],
        out_specs=pl.BlockSpec((1, tq, gw), lambda b, j, i: (b, i, j)),
        out_shape=jax.ShapeDtypeStruct((bn, seq, A_Q_W), BF16),
        scratch_shapes=[
            pltpu.VMEM((group * tq, LANES), BF16),
            pltpu.VMEM((2, group * tq, tk), F32),
            pltpu.VMEM((2, group * tq, LANES), F32),
            pltpu.VMEM((group * tq, LANES), F32),
            pltpu.VMEM((group * tq, LANES), F32),
        ],
        compiler_params=pltpu.CompilerParams(
            dimension_semantics=("parallel", "parallel", "parallel"), vmem_limit_bytes=VMEM_LIMIT),
        name="attn_a",
    )(qa, ka, va)


def _attn_b_kernel(q_ref, k_ref, v_ref, hmask_ref, bias_ref, o_ref, lse_ref,
                   s_scr, m_scr, *, sub_len, tq, tk, rb):
    nh = B_HEADS_PER_GROUP
    per_res = sub_len // tq
    n_tiles = rb * per_res
    low = lax.broadcasted_iota(jnp.int32, (tq, LANES), 1) < HEAD_DIM

    def locate(t):
        if isinstance(t, int):
            r, i = divmod(t, per_res)
            return r, i * tq, max(0, min(i * tq - B_HALF_WINDOW, sub_len - tk))
        r = t // per_res if rb > 1 else 0
        i0 = pl.multiple_of((t - r * per_res) * tq, tq)
        w0 = jnp.clip(i0 - B_HALF_WINDOW, 0, sub_len - tk)
        return r, i0, pl.multiple_of(w0, B_HALF_WINDOW)

    def lanes_to(x, width):
        return jnp.concatenate([x] * (width // LANES), axis=1)

    def stage_scores(t, slot):
        r, i0, w0 = locate(t)
        q = q_ref[0, r, pl.ds(i0, tq), :]
        k = k_ref[0, r, pl.ds(w0, tk), :]
        qs = jnp.concatenate([q * hmask_ref[h] for h in range(nh)], axis=0)
        s = lax.dot_general(qs, k, (((1,), (1,)), ((), ())), preferred_element_type=F32)
        bias = bias_ref[(i0 - w0) // B_HALF_WINDOW]
        s = s + jnp.concatenate([bias] * nh, axis=0)
        s_scr[slot] = s
        m_scr[slot] = jnp.broadcast_to(jnp.max(s, axis=1, keepdims=True), (nh * tq, LANES))

    def stage_finish(t, slot):
        r, i0, w0 = locate(t)
        v = v_ref[0, r, pl.ds(w0, tk), :]
        m = m_scr[slot]
        p = jnp.exp2(s_scr[slot] - lanes_to(m, tk))
        l = jnp.broadcast_to(jnp.sum(p, axis=1, keepdims=True), (nh * tq, LANES))
        lse = m + jnp.log(l) * LOG2_E
        pv = jnp.dot(p.astype(BF16), v, preferred_element_type=F32)
        for half in range(B_GROUP_W // LANES):
            lanes = slice(half * LANES, (half + 1) * LANES)
            ev = slice((2 * half) * tq, (2 * half + 1) * tq)
            od = slice((2 * half + 1) * tq, (2 * half + 2) * tq)
            num = jnp.where(low, pv[ev, lanes], pv[od, lanes])
            den = jnp.where(low, l[ev], l[od])
            o_ref[0, r, pl.ds(i0, tq), lanes] = (num / den).astype(BF16)
            lse_ref[0, r, pl.ds(i0, tq), lanes] = jnp.where(low, lse[ev], lse[od])

    gt = TILES_PER_STAGE
    n_groups = n_tiles // gt
    for u in range(gt):
        stage_scores(u, u)

    def step(g, carry):
        for u in range(gt):
            stage_finish(g * gt + u, u)
        for u in range(gt):
            stage_scores((g + 1) * gt + u, u)
        return carry

    lax.fori_loop(0, n_groups - 1, step, 0)
    for u in range(gt):
        stage_finish((n_groups - 1) * gt + u, u)


def _attn_b(qkv, *, tq=B_Q_TILE, min_tiles=B_MIN_TILES):
    bn, d, sub_len, _ = qkv.shape
    tq = min(tq, sub_len)
    tk = min(tq + 2 * B_HALF_WINDOW, sub_len)
    per_res = sub_len // tq
    rb = min(d, max(1, min_tiles // per_res))
    assert (rb * per_res) % TILES_PER_STAGE == 0
    rows = B_HEADS_PER_GROUP * tq
    lane_head = np.arange(B_GROUP_W) // HEAD_DIM
    hmask = lane_head[None, None, :] == np.arange(B_HEADS_PER_GROUP)[:, None, None]
    hmask = jnp.asarray(np.broadcast_to(hmask, (B_HEADS_PER_GROUP, tq, B_GROUP_W)), BF16)
    rel = np.arange(tq)[:, None] - np.arange(tk)[None, :]
    offs = np.arange(3)[:, None, None] * B_HALF_WINDOW
    bias = jnp.asarray(np.where(np.abs(rel[None] + offs) <= B_HALF_WINDOW, 0.0, MASK_VALUE), F32)

    def spec(c):
        return pl.BlockSpec((1, rb, sub_len, B_GROUP_W), lambda b, r: (b, r, 0, c))

    out_spec = pl.BlockSpec((1, rb, sub_len, B_GROUP_W), lambda b, r: (b, r, 0, 0))
    return pl.pallas_call(
        functools.partial(_attn_b_kernel, sub_len=sub_len, tq=tq, tk=tk, rb=rb),
        grid=(bn, d // rb),
        in_specs=[spec(0), spec(1), spec(2), _const_spec(hmask.shape), _const_spec(bias.shape)],
        out_specs=(out_spec, out_spec),
        out_shape=(jax.ShapeDtypeStruct((bn, d, sub_len, B_GROUP_W), BF16),
                   jax.ShapeDtypeStruct((bn, d, sub_len, B_GROUP_W), F32)),
        scratch_shapes=[
            pltpu.VMEM((TILES_PER_STAGE, rows, tk), F32),
            pltpu.VMEM((TILES_PER_STAGE, rows, LANES), F32),
        ],
        compiler_params=pltpu.CompilerParams(
            dimension_semantics=("parallel", "parallel"), vmem_limit_bytes=VMEM_LIMIT),
        name="attn_b",
    )(qkv, qkv, qkv, hmask, bias)


def _post_kernel(x_ref, ya_ref, o0_ref, l0_ref, o1_ref, l1_ref, o2_ref, l2_ref, gate_ref,
                 wa_ref, wb_ref, wo_ref, gmlp_ref, wup_ref, wdn_ref, gfin_ref,
                 y_ref, ril_ref, *, tn, ff_chunk, final_norm):
    dm = x_ref.shape[-1]
    n_slab = B_GROUP_W // LANES

    def to_natural_order(src_ref, d, slot):
        for r in range(d):
            blk = src_ref[0, r].astype(F32)
            for s in range(n_slab):
                ril_ref[slot * n_slab + s, pl.ds(r, tn // d, stride=d), :] = blk[:, s * LANES:(s + 1) * LANES]

    to_natural_order(o1_ref, B_DILATIONS[1], 0)
    to_natural_order(l1_ref, B_DILATIONS[1], 1)
    to_natural_order(o2_ref, B_DILATIONS[2], 2)
    to_natural_order(l2_ref, B_DILATIONS[2], 3)

    def natural(slot):
        return jnp.concatenate([ril_ref[slot * n_slab + s] for s in range(n_slab)], axis=1)

    o0 = o0_ref[0, 0].astype(F32)
    l0 = l0_ref[0, 0]
    o1, l1, o2, l2 = (natural(slot) for slot in range(4))
    mx = jnp.maximum(jnp.maximum(l0, l1), l2)
    e0, e1, e2 = jnp.exp2(l0 - mx), jnp.exp2(l1 - mx), jnp.exp2(l2 - mx)
    yb_in = (e0 * o0 + e1 * o1 + e2 * o2) / (e0 + e1 + e2)

    ya = jnp.dot(ya_ref[0], wa_ref[...], preferred_element_type=F32)
    yb = jnp.dot(yb_in.astype(BF16), wb_ref[...], preferred_element_type=F32)
    ga = gate_ref[0, :, :dm].astype(F32)
    gb = gate_ref[0, :, dm:].astype(F32)
    mixed = ya * ga + yb * gb
    x1 = x_ref[0] + jnp.dot(mixed.astype(BF16), wo_ref[...], preferred_element_type=F32)

    ms = jnp.mean(x1 * x1, axis=-1, keepdims=True)
    h2 = (x1 * lax.rsqrt(ms + EPS) * gmlp_ref[...]).astype(BF16)
    acc = x1
    d_ff = wup_ref.shape[1]
    for c in range(d_ff // ff_chunk):
        u = jnp.dot(h2, wup_ref[:, c * ff_chunk:(c + 1) * ff_chunk], preferred_element_type=F32)
        u = jnp.square(jnp.maximum(u, 0.0)).astype(BF16)
        acc = acc + jnp.dot(u, wdn_ref[c * ff_chunk:(c + 1) * ff_chunk, :],
                            preferred_element_type=F32)
    if final_norm:
        ms = jnp.mean(acc * acc, axis=-1, keepdims=True)
        acc = acc * lax.rsqrt(ms + EPS) * gfin_ref[...]
    y_ref[0] = acc


def _post(x, ya, b_outs, gates, w_a_out, w_b_out, w_out, g_mlp, w_up, w_down, g_final,
          *, tn, final_norm, ff_chunk=MLP_CHUNK):
    bn, seq, dm = x.shape
    d_ff = w_up.shape[1]
    grid = (bn, seq // tn)
    in_specs = [
        pl.BlockSpec((1, tn, dm), lambda b, i: (b, i, 0)),
        pl.BlockSpec((1, tn, A_Q_W), lambda b, i: (b, i, 0)),
    ]
    operands = [x, ya]
    for d, (o, lse) in zip(B_DILATIONS, b_outs):
        spec = pl.BlockSpec((1, d, tn // d, B_GROUP_W), lambda b, i: (b, 0, i, 0))
        in_specs += [spec, spec]
        operands += [o, lse]
    in_specs += [
        pl.BlockSpec((1, tn, gates.shape[-1]), lambda b, i: (b, i, 0)),
        _const_spec(w_a_out.shape), _const_spec(w_b_out.shape), _const_spec(w_out.shape),
        _const_spec((1, dm)), _const_spec(w_up.shape), _const_spec(w_down.shape), _const_spec((1, dm)),
    ]
    operands += [gates, w_a_out, w_b_out, w_out, g_mlp, w_up, w_down, g_final]
    n_ril = 4 * (B_GROUP_W // LANES)
    return pl.pallas_call(
        functools.partial(_post_kernel, tn=tn, ff_chunk=ff_chunk, final_norm=final_norm),
        grid=grid,
        in_specs=in_specs,
        out_specs=pl.BlockSpec((1, tn, dm), lambda b, i: (b, i, 0)),
        out_shape=jax.ShapeDtypeStruct((bn, seq, dm), F32),
        scratch_shapes=[pltpu.VMEM((n_ril, tn, LANES), F32)],
        compiler_params=pltpu.CompilerParams(
            dimension_semantics=("parallel", "parallel"), vmem_limit_bytes=VMEM_LIMIT),
        name="post",
    )(*operands)


def _layer(x, w, tabs, *, final_norm, g_final):
    tab_a, tab_bq, tab_bk = tabs
    seq = x.shape[1]
    assert seq % TOKEN_TILE == 0 and seq % A_Q_TILE == 0 and seq % A_K_TILE == 0
    assert TOKEN_TILE % (BF16_ROWS * max(B_DILATIONS)) == 0
    qa, ka, va, b0, b1, b2, gates = _inproj(
        x, w["w_in"], w["g_mix"], w["q_gain"], w["k_gain"], tab_a, tab_bq, tab_bk, tn=TOKEN_TILE)
    ya = _attn_a(qa, ka, va, tq=A_Q_TILE, tk=A_K_TILE)
    b_outs = [_attn_b(b) for b in (b0, b1, b2)]
    return _post(x, ya, b_outs, gates, w["w_a_out"], w["w_b_out"], w["w_out"], w["g_mlp"],
                 w["w_up"], w["w_down"], g_final, tn=TOKEN_TILE, final_norm=final_norm)


def _trunk(x, layers, g_final):
    seq = x.shape[1]
    tab_a, tab_b = _rope_tables(seq)
    tabs = tuple(jnp.asarray(t, F32) for t in (tab_a, tab_b * Q_SCALE, tab_b))
    for i, w in enumerate(layers):
        x = _layer(x, w, tabs, final_norm=(i == len(layers) - 1), g_final=g_final)
    return x


def kernel(x_prompt, x_sample, w_in, w_a_out, w_b_out, w_out, g_mix, q_gain, k_gain, g_mlp,
           w_up, w_down, g_final):
    depth = w_in.shape[0]
    reps = LANES // HEAD_DIM
    layers = []
    for l in range(depth):
        layers.append(dict(
            w_in=w_in[l].astype(BF16), w_a_out=w_a_out[l].astype(BF16),
            w_b_out=w_b_out[l].astype(BF16), w_out=w_out[l].astype(BF16),
            w_up=w_up[l].astype(BF16), w_down=w_down[l].astype(BF16),
            g_mix=g_mix[l][None, :], g_mlp=g_mlp[l][None, :],
            q_gain=jnp.tile(q_gain[l] * Q_SCALE, reps)[None, :],
            k_gain=jnp.tile(k_gain[l], reps)[None, :],
        ))
    gf = g_final[None, :]
    return (_trunk(x_prompt, layers, gf), _trunk(x_sample, layers, gf))
```

```python
import functools
import math

import jax
import jax.numpy as jnp
import numpy as np
from jax import lax
from jax.experimental import pallas as pl
from jax.experimental.pallas import tpu as pltpu

F32 = jnp.float32
BF16 = jnp.bfloat16

HEAD_DIM = 64
A_Q_HEADS = 8
A_KV_HEADS = 2
B_DILATIONS = (1, 4, 16)
B_HALF_WINDOW = 64
B_HEADS_PER_GROUP = 4
GRID_W = 64
AXIAL_THETA = 10000.0
PARTIAL_THETA = 500000.0
PARTIAL_ROPE_DIM = HEAD_DIM // 4
EPS = 1e-6
MASK_VALUE = -1e30

LANES = 128
BF16_ROWS = 16
A_Q_W = A_Q_HEADS * HEAD_DIM
A_KV_W = A_KV_HEADS * HEAD_DIM
B_GROUP_W = B_HEADS_PER_GROUP * HEAD_DIM
B_W = len(B_DILATIONS) * B_GROUP_W
LOG2_E = math.log2(math.e)
Q_SCALE = HEAD_DIM ** -0.5 * LOG2_E

TOKEN_TILE = 512
PROJ_CHUNK = 512
MLP_CHUNK = 2048
A_Q_TILE = 1024
A_K_TILE = 1024
B_Q_TILE = 128
B_MIN_TILES = 16
B_INTERLEAVE = 8
VMEM_LIMIT = 56 * 1024 * 1024


def _const_spec(shape):
    return pl.BlockSpec(shape, lambda *_: (0,) * len(shape), pipeline_mode=pl.Buffered(1))


def _rope_tables(seq):
    t = np.arange(seq)

    def half_tables(pos, d, theta):
        d2 = d // 2
        freqs = theta ** (-(np.arange(d2, dtype=np.float64) * 2.0) / d)
        ang = pos.astype(np.float64)[:, None] * freqs[None, :]
        cos, sin = np.cos(ang), np.sin(ang)
        zero = np.zeros_like(sin)
        return (np.concatenate([cos, cos], -1), np.concatenate([-sin, zero], -1),
                np.concatenate([zero, sin], -1))

    half = HEAD_DIM // 2
    row = half_tables(t // GRID_W, half, AXIAL_THETA)
    col = half_tables(t % GRID_W, half, AXIAL_THETA)
    axial = [np.concatenate([r, c], -1) for r, c in zip(row, col)]
    part = half_tables(t, PARTIAL_ROPE_DIM, PARTIAL_THETA)
    rest = HEAD_DIM - PARTIAL_ROPE_DIM
    fill = (np.ones((seq, rest)), np.zeros((seq, rest)), np.zeros((seq, rest)))
    partial = [np.concatenate([p, f], -1) for p, f in zip(part, fill)]
    reps = LANES // HEAD_DIM
    tab_a = np.stack([np.tile(a, (1, reps)) for a in axial])
    tab_b = np.stack([np.tile(p, (1, reps)) for p in partial])
    return tab_a, tab_b


def _inproj_kernel(x_ref, gmix_ref, w_ref, qg_ref, kg_ref, seg_ref, ta_ref, tbq_ref, tbk_ref,
                   qa_ref, ka_ref, va_ref, b0_ref, b1_ref, b2_ref, gate_ref, z_ref, dl_ref, *, tn):
    x = x_ref[0]
    ms = jnp.mean(x * x, axis=-1, keepdims=True)
    h = (x * lax.rsqrt(ms + EPS) * gmix_ref[...]).astype(BF16)
    low = lax.broadcasted_iota(jnp.int32, (tn, LANES), 1) < HEAD_DIM
    n_slab = PROJ_CHUNK // LANES
    n_chunk = w_ref.shape[1] // PROJ_CHUNK
    first_gate = (A_Q_W + 2 * A_KV_W + 3 * B_W) // PROJ_CHUNK
    order = list(range(first_gate, n_chunk)) + list(range(first_gate))
    slot_of = {c: pos % 2 for pos, c in enumerate(order)}

    def stage(c):
        z = jnp.dot(h, w_ref[:, c * PROJ_CHUNK:(c + 1) * PROJ_CHUNK], preferred_element_type=F32)
        for s in range(n_slab):
            z_ref[slot_of[c], s] = z[:, s * LANES:(s + 1) * LANES]

    def slab(col):
        c, rem = divmod(col, PROJ_CHUNK)
        return z_ref.at[slot_of[c], rem // LANES]

    def inv_rms(zs, seg):
        z2 = jnp.concatenate([z * z for z in zs], axis=1).astype(BF16)
        ss = jnp.dot(z2, seg, preferred_element_type=F32)
        return lax.rsqrt(ss * (1.0 / HEAD_DIM) + EPS)

    def rope(z, tab_ref, sh):
        return (z * tab_ref[0] + pltpu.roll(z, LANES - sh, 1) * tab_ref[1]
                + pltpu.roll(z, sh, 1) * tab_ref[2])

    a_sh = HEAD_DIM // 4
    b_sh = PARTIAL_ROPE_DIM // 2
    kv_col = A_Q_W
    b_col = A_Q_W + 2 * A_KV_W
    g_col = b_col + 3 * B_W
    b_refs = (b0_ref, b1_ref, b2_ref)
    dl_slot = [0]

    def epilogue_a_q(col):
        zs = [slab(col)[...], slab(col + LANES)[...]]
        inv = inv_rms(zs, seg_ref[...])
        for i, z in enumerate(zs):
            zn = z * inv[:, i * LANES:(i + 1) * LANES] * qg_ref[...]
            c0 = col + i * LANES
            qa_ref[0, :, c0:c0 + LANES] = rope(zn, ta_ref, a_sh).astype(BF16)

    def epilogue_a_kv(col):
        zk = slab(col)[...]
        zk = rope(zk * inv_rms([zk], seg_ref[:LANES, :LANES]) * kg_ref[...], ta_ref, a_sh)
        zkr = pltpu.roll(zk, HEAD_DIM, 1)
        ka_ref[0, 0] = jnp.where(low, zk, zkr).astype(BF16)
        ka_ref[0, 1] = jnp.where(low, zkr, zk).astype(BF16)
        zv = slab(col + LANES)[...]
        va_ref[0, 0] = jnp.where(low, zv, 1.0).astype(BF16)
        va_ref[0, 1] = jnp.where(low, pltpu.roll(zv, HEAD_DIM, 1), 1.0).astype(BF16)

    def epilogue_b(col):
        t, rem = divmod(col - b_col, B_W)
        g, rem = divmod(rem, B_GROUP_W)
        d = B_DILATIONS[g]
        out_col = t * B_GROUP_W + rem
        tab_ref = (tbq_ref, tbk_ref, None)[t]
        src = slab(col)
        if tab_ref is not None:
            z = rope(src[...], tab_ref, b_sh)
            if d == 1:
                b_refs[g][0, 0, :, out_col:out_col + LANES] = z.astype(BF16)
                return
            src = dl_ref.at[dl_slot[0]]
            dl_slot[0] = 1 - dl_slot[0]
            src[...] = z
        elif d == 1:
            b_refs[g][0, 0, :, out_col:out_col + LANES] = src[...].astype(BF16)
            return
        for r in range(d):
            piece = src[pl.ds(r, tn // d, stride=d), :]
            b_refs[g][0, r, :, out_col:out_col + LANES] = piece.astype(BF16)

    def epilogue(c):
        col = c * PROJ_CHUNK
        while col < (c + 1) * PROJ_CHUNK:
            if col < kv_col:
                epilogue_a_q(col)
                col += 2 * LANES
            elif col < b_col:
                epilogue_a_kv(col)
                col += 2 * LANES
            elif col < g_col:
                epilogue_b(col)
                col += LANES
            else:
                gate = 1.0 / (1.0 + jnp.exp(-slab(col)[...]))
                gate_ref[0, :, col - g_col:col - g_col + LANES] = gate.astype(BF16)
                col += LANES

    stage(order[0])
    for pos, c in enumerate(order):
        if pos + 1 < n_chunk:
            stage(order[pos + 1])
        epilogue(c)


def _inproj(x, w_in, g_mix, q_gain, k_gain, tab_a, tab_bq, tab_bk, *, tn):
    bn, seq, dm = x.shape
    in_w = w_in.shape[1]
    gate_w = in_w - (A_Q_W + 2 * A_KV_W + 3 * B_W)
    grid = (bn, seq // tn)
    tab_spec = pl.BlockSpec((3, tn, LANES), lambda b, i: (0, i, 0))
    head_of = np.arange(2 * LANES) // HEAD_DIM
    seg = jnp.asarray(head_of[:, None] == head_of[None, :], BF16)
    out_shape = (
        jax.ShapeDtypeStruct((bn, seq, A_Q_W), BF16),
        jax.ShapeDtypeStruct((bn, A_KV_HEADS, seq, LANES), BF16),
        jax.ShapeDtypeStruct((bn, A_KV_HEADS, seq, LANES), BF16),
    ) + tuple(jax.ShapeDtypeStruct((bn, d, seq // d, 3 * B_GROUP_W), BF16) for d in B_DILATIONS) + (
        jax.ShapeDtypeStruct((bn, seq, gate_w), BF16),
    )
    out_specs = (
        pl.BlockSpec((1, tn, A_Q_W), lambda b, i: (b, i, 0)),
        pl.BlockSpec((1, A_KV_HEADS, tn, LANES), lambda b, i: (b, 0, i, 0)),
        pl.BlockSpec((1, A_KV_HEADS, tn, LANES), lambda b, i: (b, 0, i, 0)),
    ) + tuple(pl.BlockSpec((1, d, tn // d, 3 * B_GROUP_W), lambda b, i: (b, 0, i, 0))
              for d in B_DILATIONS) + (
        pl.BlockSpec((1, tn, gate_w), lambda b, i: (b, i, 0)),
    )
    return pl.pallas_call(
        functools.partial(_inproj_kernel, tn=tn),
        grid=grid,
        in_specs=[
            pl.BlockSpec((1, tn, dm), lambda b, i: (b, i, 0)),
            _const_spec((1, dm)),
            _const_spec((dm, in_w)),
            _const_spec((1, LANES)),
            _const_spec((1, LANES)),
            _const_spec(seg.shape),
            tab_spec, tab_spec, tab_spec,
        ],
        out_specs=out_specs,
        out_shape=out_shape,
        scratch_shapes=[pltpu.VMEM((2, PROJ_CHUNK // LANES, tn, LANES), F32),
                        pltpu.VMEM((2, tn, LANES), F32)],
        compiler_params=pltpu.CompilerParams(
            dimension_semantics=("parallel", "parallel"), vmem_limit_bytes=VMEM_LIMIT),
        name="inproj",
    )(x, g_mix, w_in, q_gain, k_gain, seg, tab_a, tab_bq, tab_bk)


def _attn_a_kernel(q_ref, k_ref, v_ref, o_ref, qs_ref, s_ref, pm_ref, m_ref, acc_ref,
                   *, tq, tk, nk):
    group = A_Q_HEADS // A_KV_HEADS
    n_lc = tk // LANES
    lane = lax.broadcasted_iota(jnp.int32, (tq, LANES), 1)
    low = lane < HEAD_DIM
    for h in range(group):
        pair = q_ref[0, :, (h // 2) * LANES:(h // 2 + 1) * LANES].astype(F32)
        keep = low if h % 2 == 0 else jnp.logical_not(low)
        qs_ref[h * tq:(h + 1) * tq, :] = jnp.where(keep, pair, 0.0).astype(BF16)
    m_ref[...] = jnp.full(m_ref.shape, MASK_VALUE, F32)
    acc_ref[...] = jnp.zeros(acc_ref.shape, F32)

    def scores(c):
        k = k_ref[0, 0, c * tk:(c + 1) * tk, :]
        s = lax.dot_general(qs_ref[...], k, (((1,), (1,)), ((), ())),
                            preferred_element_type=F32)
        s_ref[c % 2] = s
        pm = s[:, :LANES]
        for j in range(1, n_lc):
            pm = jnp.maximum(pm, s[:, j * LANES:(j + 1) * LANES])
        pm_ref[c % 2] = pm

    scores(0)
    for c in range(nk):
        if c + 1 < nk:
            scores(c + 1)
        slot = c % 2
        m_prev = m_ref[...]
        m_next = jnp.maximum(m_prev, jnp.max(pm_ref[slot], axis=1, keepdims=True))
        alpha = jnp.exp2(m_prev - m_next)
        m_ref[...] = m_next
        p = jnp.concatenate(
            [jnp.exp2(s_ref[slot, :, j * LANES:(j + 1) * LANES] - m_next) for j in range(n_lc)],
            axis=1).astype(BF16)
        v = v_ref[0, 0, c * tk:(c + 1) * tk, :]
        acc_ref[...] = alpha * acc_ref[...] + jnp.dot(p, v, preferred_element_type=F32)

    acc = acc_ref[...]
    acc_sw = pltpu.roll(acc, HEAD_DIM, 1)
    for pr in range(group // 2):
        ev = slice((2 * pr) * tq, (2 * pr + 1) * tq)
        od = slice((2 * pr + 1) * tq, (2 * pr + 2) * tq)
        out = jnp.where(low, acc[ev] / acc_sw[ev], acc_sw[od] / acc[od])
        o_ref[0, :, pr * LANES:(pr + 1) * LANES] = out.astype(BF16)


def _attn_a(qa, ka, va, *, tq, tk):
    bn, seq, _ = qa.shape
    group = A_Q_HEADS // A_KV_HEADS
    gw = group * HEAD_DIM
    grid = (bn, A_KV_HEADS, seq // tq)
    kv_spec = pl.BlockSpec((1, 1, seq, LANES), lambda b, j, i: (b, j, 0, 0))
    return pl.pallas_call(
        functools.partial(_attn_a_kernel, tq=tq, tk=tk, nk=seq // tk),
        grid=grid,
        in_specs=[pl.BlockSpec((1, tq, gw), lambda b, j, i: (b, i, j)), kv_spec, kv_spec],
        out_specs=pl.BlockSpec((1, tq, gw), lambda b, j, i: (b, i, j)),
        out_shape=jax.ShapeDtypeStruct((bn, seq, A_Q_W), BF16),
        scratch_shapes=[
            pltpu.VMEM((group * tq, LANES), BF16),
            pltpu.VMEM((2, group * tq, tk), F32),
            pltpu.VMEM((2, group * tq, LANES), F32),
            pltpu.VMEM((group * tq, LANES), F32),
            pltpu.VMEM((group * tq, LANES), F32),
        ],
        compiler_params=pltpu.CompilerParams(
            dimension_semantics=("parallel", "parallel", "parallel"), vmem_limit_bytes=VMEM_LIMIT),
        name="attn_a",
    )(qa, ka, va)


def _attn_b_kernel(q_ref, k_ref, v_ref, hmask_ref, bias_ref, o_ref, lse_ref,
                   *, sub_len, tq, tk, rb):
    nh = B_HEADS_PER_GROUP
    per_res = sub_len // tq
    n_tiles = rb * per_res
    low = lax.broadcasted_iota(jnp.int32, (tq, LANES), 1) < HEAD_DIM

    def locate(t):
        if isinstance(t, int):
            r, i = divmod(t, per_res)
            return r, i * tq, max(0, min(i * tq - B_HALF_WINDOW, sub_len - tk))
        r = t // per_res if rb > 1 else 0
        i0 = pl.multiple_of((t - r * per_res) * tq, tq)
        w0 = jnp.clip(i0 - B_HALF_WINDOW, 0, sub_len - tk)
        return r, i0, pl.multiple_of(w0, B_HALF_WINDOW)

    def tile(t):
        r, i0, w0 = locate(t)
        q = q_ref[0, r, pl.ds(i0, tq), :]
        k = k_ref[0, r, pl.ds(w0, tk), :]
        v = v_ref[0, r, pl.ds(w0, tk), :]
        qs = jnp.concatenate([q * hmask_ref[h] for h in range(nh)], axis=0)
        s = lax.dot_general(qs, k, (((1,), (1,)), ((), ())), preferred_element_type=F32)
        bias = bias_ref[(i0 - w0) // B_HALF_WINDOW]
        s = s + jnp.concatenate([bias] * nh, axis=0)
        m = jnp.max(s, axis=1, keepdims=True)
        p = jnp.exp2(s - m)
        l = jnp.broadcast_to(jnp.sum(p, axis=1, keepdims=True), (nh * tq, LANES))
        lse = jnp.broadcast_to(m, (nh * tq, LANES)) + jnp.log(l) * LOG2_E
        pv = jnp.dot(p.astype(BF16), v, preferred_element_type=F32)
        for half in range(B_GROUP_W // LANES):
            lanes = slice(half * LANES, (half + 1) * LANES)
            ev = slice((2 * half) * tq, (2 * half + 1) * tq)
            od = slice((2 * half + 1) * tq, (2 * half + 2) * tq)
            num = jnp.where(low, pv[ev, lanes], pv[od, lanes])
            den = jnp.where(low, l[ev], l[od])
            o_ref[0, r, pl.ds(i0, tq), lanes] = (num / den).astype(BF16)
            lse_ref[0, r, pl.ds(i0, tq), lanes] = jnp.where(low, lse[ev], lse[od])

    def group(g, carry):
        for u in range(B_INTERLEAVE):
            tile(g * B_INTERLEAVE + u)
        return carry

    lax.fori_loop(0, n_tiles // B_INTERLEAVE, group, 0)


def _attn_b(qkv, *, tq=B_Q_TILE, min_tiles=B_MIN_TILES):
    bn, d, sub_len, _ = qkv.shape
    tq = min(tq, sub_len)
    tk = min(tq + 2 * B_HALF_WINDOW, sub_len)
    per_res = sub_len // tq
    rb = min(d, max(1, min_tiles // per_res))
    assert (rb * per_res) % B_INTERLEAVE == 0
    lane_head = np.arange(B_GROUP_W) // HEAD_DIM
    hmask = lane_head[None, None, :] == np.arange(B_HEADS_PER_GROUP)[:, None, None]
    hmask = jnp.asarray(np.broadcast_to(hmask, (B_HEADS_PER_GROUP, tq, B_GROUP_W)), BF16)
    rel = np.arange(tq)[:, None] - np.arange(tk)[None, :]
    offs = np.arange(3)[:, None, None] * B_HALF_WINDOW
    bias = jnp.asarray(np.where(np.abs(rel[None] + offs) <= B_HALF_WINDOW, 0.0, MASK_VALUE), F32)

    def spec(c):
        return pl.BlockSpec((1, rb, sub_len, B_GROUP_W), lambda b, r: (b, r, 0, c))

    out_spec = pl.BlockSpec((1, rb, sub_len, B_GROUP_W), lambda b, r: (b, r, 0, 0))
    return pl.pallas_call(
        functools.partial(_attn_b_kernel, sub_len=sub_len, tq=tq, tk=tk, rb=rb),
        grid=(bn, d // rb),
        in_specs=[spec(0), spec(1), spec(2), _const_spec(hmask.shape), _const_spec(bias.shape)],
        out_specs=(out_spec, out_spec),
        out_shape=(jax.ShapeDtypeStruct((bn, d, sub_len, B_GROUP_W), BF16),
                   jax.ShapeDtypeStruct((bn, d, sub_len, B_GROUP_W), F32)),
        compiler_params=pltpu.CompilerParams(
            dimension_semantics=("parallel", "parallel"), vmem_limit_bytes=VMEM_LIMIT),
        name="attn_b",
    )(qkv, qkv, qkv, hmask, bias)


def _post_kernel(x_ref, ya_ref, o0_ref, l0_ref, o1_ref, l1_ref, o2_ref, l2_ref, gate_ref,
                 wa_ref, wb_ref, wo_ref, gmlp_ref, wup_ref, wdn_ref, gfin_ref,
                 y_ref, ril_ref, *, tn, ff_chunk, final_norm):
    dm = x_ref.shape[-1]
    n_slab = B_GROUP_W // LANES

    def to_natural_order(src_ref, d, slot):
        for r in range(d):
            blk = src_ref[0, r].astype(F32)
            for s in range(n_slab):
                ril_ref[slot * n_slab + s, pl.ds(r, tn // d, stride=d), :] = blk[:, s * LANES:(s + 1) * LANES]

    to_natural_order(o1_ref, B_DILATIONS[1], 0)
    to_natural_order(l1_ref, B_DILATIONS[1], 1)
    to_natural_order(o2_ref, B_DILATIONS[2], 2)
    to_natural_order(l2_ref, B_DILATIONS[2], 3)

    def natural(slot):
        return jnp.concatenate([ril_ref[slot * n_slab + s] for s in range(n_slab)], axis=1)

    o0 = o0_ref[0, 0].astype(F32)
    l0 = l0_ref[0, 0]
    o1, l1, o2, l2 = (natural(slot) for slot in range(4))
    mx = jnp.maximum(jnp.maximum(l0, l1), l2)
    e0, e1, e2 = jnp.exp2(l0 - mx), jnp.exp2(l1 - mx), jnp.exp2(l2 - mx)
    yb_in = (e0 * o0 + e1 * o1 + e2 * o2) / (e0 + e1 + e2)

    ya = jnp.dot(ya_ref[0], wa_ref[...], preferred_element_type=F32)
    yb = jnp.dot(yb_in.astype(BF16), wb_ref[...], preferred_element_type=F32)
    ga = gate_ref[0, :, :dm].astype(F32)
    gb = gate_ref[0, :, dm:].astype(F32)
    mixed = ya * ga + yb * gb
    x1 = x_ref[0] + jnp.dot(mixed.astype(BF16), wo_ref[...], preferred_element_type=F32)

    ms = jnp.mean(x1 * x1, axis=-1, keepdims=True)
    h2 = (x1 * lax.rsqrt(ms + EPS) * gmlp_ref[...]).astype(BF16)
    acc = x1
    d_ff = wup_ref.shape[1]
    for c in range(d_ff // ff_chunk):
        u = jnp.dot(h2, wup_ref[:, c * ff_chunk:(c + 1) * ff_chunk], preferred_element_type=F32)
        u = jnp.square(jnp.maximum(u, 0.0)).astype(BF16)
        acc = acc + jnp.dot(u, wdn_ref[c * ff_chunk:(c + 1) * ff_chunk, :],
                            preferred_element_type=F32)
    if final_norm:
        ms = jnp.mean(acc * acc, axis=-1, keepdims=True)
        acc = acc * lax.rsqrt(ms + EPS) * gfin_ref[...]
    y_ref[0] = acc


def _post(x, ya, b_outs, gates, w_a_out, w_b_out, w_out, g_mlp, w_up, w_down, g_final,
          *, tn, final_norm, ff_chunk=MLP_CHUNK):
    bn, seq, dm = x.shape
    d_ff = w_up.shape[1]
    grid = (bn, seq // tn)
    in_specs = [
        pl.BlockSpec((1, tn, dm), lambda b, i: (b, i, 0)),
        pl.BlockSpec((1, tn, A_Q_W), lambda b, i: (b, i, 0)),
    ]
    operands = [x, ya]
    for d, (o, lse) in zip(B_DILATIONS, b_outs):
        spec = pl.BlockSpec((1, d, tn // d, B_GROUP_W), lambda b, i: (b, 0, i, 0))
        in_specs += [spec, spec]
        operands += [o, lse]
    in_specs += [
        pl.BlockSpec((1, tn, gates.shape[-1]), lambda b, i: (b, i, 0)),
        _const_spec(w_a_out.shape), _const_spec(w_b_out.shape), _const_spec(w_out.shape),
        _const_spec((1, dm)), _const_spec(w_up.shape), _const_spec(w_down.shape), _const_spec((1, dm)),
    ]
    operands += [gates, w_a_out, w_b_out, w_out, g_mlp, w_up, w_down, g_final]
    n_ril = 4 * (B_GROUP_W // LANES)
    return pl.pallas_call(
        functools.partial(_post_kernel, tn=tn, ff_chunk=ff_chunk, final_norm=final_norm),
        grid=grid,
        in_specs=in_specs,
        out_specs=pl.BlockSpec((1, tn, dm), lambda b, i: (b, i, 0)),
        out_shape=jax.ShapeDtypeStruct((bn, seq, dm), F32),
        scratch_shapes=[pltpu.VMEM((n_ril, tn, LANES), F32)],
        compiler_params=pltpu.CompilerParams(
            dimension_semantics=("parallel", "parallel"), vmem_limit_bytes=VMEM_LIMIT),
        name="post",
    )(*operands)


def _layer(x, w, tabs, *, final_norm, g_final):
    tab_a, tab_bq, tab_bk = tabs
    seq = x.shape[1]
    assert seq % TOKEN_TILE == 0 and seq % A_Q_TILE == 0 and seq % A_K_TILE == 0
    assert TOKEN_TILE % (BF16_ROWS * max(B_DILATIONS)) == 0
    qa, ka, va, b0, b1, b2, gates = _inproj(
        x, w["w_in"], w["g_mix"], w["q_gain"], w["k_gain"], tab_a, tab_bq, tab_bk, tn=TOKEN_TILE)
    ya = _attn_a(qa, ka, va, tq=A_Q_TILE, tk=A_K_TILE)
    b_outs = [_attn_b(b) for b in (b0, b1, b2)]
    return _post(x, ya, b_outs, gates, w["w_a_out"], w["w_b_out"], w["w_out"], w["g_mlp"],
                 w["w_up"], w["w_down"], g_final, tn=TOKEN_TILE, final_norm=final_norm)


def _trunk(x, layers, g_final):
    seq = x.shape[1]
    tab_a, tab_b = _rope_tables(seq)
    tabs = tuple(jnp.asarray(t, F32) for t in (tab_a, tab_b * Q_SCALE, tab_b))
    for i, w in enumerate(layers):
        x = _layer(x, w, tabs, final_norm=(i == len(layers) - 1), g_final=g_final)
    return x


def kernel(x_prompt, x_sample, w_in, w_a_out, w_b_out, w_out, g_mix, q_gain, k_gain, g_mlp,
           w_up, w_down, g_final):
    depth = w_in.shape[0]
    reps = LANES // HEAD_DIM
    layers = []
    for l in range(depth):
        layers.append(dict(
            w_in=w_in[l].astype(BF16), w_a_out=w_a_out[l].astype(BF16),
            w_b_out=w_b_out[l].astype(BF16), w_out=w_out[l].astype(BF16),
            w_up=w_up[l].astype(BF16), w_down=w_down[l].astype(BF16),
            g_mix=g_mix[l][None, :], g_mlp=g_mlp[l][None, :],
            q_gain=jnp.tile(q_gain[l] * Q_SCALE, reps)[None, :],
            k_gain=jnp.tile(k_gain[l], reps)[None, :],
        ))
    gf = g_final[None, :]
    return (_trunk(x_prompt, layers, gf), _trunk(x_sample, layers, gf))
```

```python
import functools
import math

import jax
import jax.numpy as jnp
import numpy as np
from jax import lax
from jax.experimental import pallas as pl
from jax.experimental.pallas import tpu as pltpu

F32 = jnp.float32
BF16 = jnp.bfloat16

HEAD_DIM = 64
A_Q_HEADS = 8
A_KV_HEADS = 2
B_DILATIONS = (1, 4, 16)
B_HALF_WINDOW = 64
B_HEADS_PER_GROUP = 4
GRID_W = 64
AXIAL_THETA = 10000.0
PARTIAL_THETA = 500000.0
PARTIAL_ROPE_DIM = HEAD_DIM // 4
EPS = 1e-6
MASK_VALUE = -1e30

LANES = 128
BF16_ROWS = 16
A_Q_W = A_Q_HEADS * HEAD_DIM
A_KV_W = A_KV_HEADS * HEAD_DIM
B_GROUP_W = B_HEADS_PER_GROUP * HEAD_DIM
B_W = len(B_DILATIONS) * B_GROUP_W
LOG2_E = math.log2(math.e)
Q_SCALE = HEAD_DIM ** -0.5 * LOG2_E

TOKEN_TILE = 512
PROJ_CHUNK = 512
MLP_CHUNK = 512
A_Q_TILE = 1024
A_K_TILE = 1024
B_Q_TILE = 128
B_MIN_TILES = 16
B_INTERLEAVE = 16
VMEM_LIMIT = 56 * 1024 * 1024


def _const_spec(shape):
    return pl.BlockSpec(shape, lambda *_: (0,) * len(shape), pipeline_mode=pl.Buffered(1))


def _rope_tables(seq):
    t = np.arange(seq)

    def half_tables(pos, d, theta):
        d2 = d // 2
        freqs = theta ** (-(np.arange(d2, dtype=np.float64) * 2.0) / d)
        ang = pos.astype(np.float64)[:, None] * freqs[None, :]
        cos, sin = np.cos(ang), np.sin(ang)
        zero = np.zeros_like(sin)
        return (np.concatenate([cos, cos], -1), np.concatenate([-sin, zero], -1),
                np.concatenate([zero, sin], -1))

    half = HEAD_DIM // 2
    row = half_tables(t // GRID_W, half, AXIAL_THETA)
    col = half_tables(t % GRID_W, half, AXIAL_THETA)
    axial = [np.concatenate([r, c], -1) for r, c in zip(row, col)]
    part = half_tables(t, PARTIAL_ROPE_DIM, PARTIAL_THETA)
    rest = HEAD_DIM - PARTIAL_ROPE_DIM
    fill = (np.ones((seq, rest)), np.zeros((seq, rest)), np.zeros((seq, rest)))
    partial = [np.concatenate([p, f], -1) for p, f in zip(part, fill)]
    reps = LANES // HEAD_DIM
    tab_a = np.stack([np.tile(a, (1, reps)) for a in axial])
    tab_b = np.stack([np.tile(p, (1, reps)) for p in partial])
    return tab_a, tab_b


def _inproj_kernel(x_ref, gmix_ref, w_ref, qg_ref, kg_ref, seg_ref, ta_ref, tbq_ref, tbk_ref,
                   qa_ref, ka_ref, va_ref, b0_ref, b1_ref, b2_ref, gate_ref, z_ref, dl_ref, *, tn):
    x = x_ref[0]
    ms = jnp.mean(x * x, axis=-1, keepdims=True)
    h = (x * lax.rsqrt(ms + EPS) * gmix_ref[...]).astype(BF16)
    low = lax.broadcasted_iota(jnp.int32, (tn, LANES), 1) < HEAD_DIM
    n_slab = PROJ_CHUNK // LANES
    n_chunk = w_ref.shape[1] // PROJ_CHUNK
    first_gate = (A_Q_W + 2 * A_KV_W + 3 * B_W) // PROJ_CHUNK
    order = list(range(first_gate, n_chunk)) + list(range(first_gate))
    slot_of = {c: pos % 2 for pos, c in enumerate(order)}

    def stage(c):
        z = jnp.dot(h, w_ref[:, c * PROJ_CHUNK:(c + 1) * PROJ_CHUNK], preferred_element_type=F32)
        for s in range(n_slab):
            z_ref[slot_of[c], s] = z[:, s * LANES:(s + 1) * LANES]

    def slab(col):
        c, rem = divmod(col, PROJ_CHUNK)
        return z_ref.at[slot_of[c], rem // LANES]

    def inv_rms(zs, seg):
        z2 = jnp.concatenate([z * z for z in zs], axis=1).astype(BF16)
        ss = jnp.dot(z2, seg, preferred_element_type=F32)
        return lax.rsqrt(ss * (1.0 / HEAD_DIM) + EPS)

    def rope(z, tab_ref, sh):
        return (z * tab_ref[0] + pltpu.roll(z, LANES - sh, 1) * tab_ref[1]
                + pltpu.roll(z, sh, 1) * tab_ref[2])

    a_sh = HEAD_DIM // 4
    b_sh = PARTIAL_ROPE_DIM // 2
    kv_col = A_Q_W
    b_col = A_Q_W + 2 * A_KV_W
    g_col = b_col + 3 * B_W
    b_refs = (b0_ref, b1_ref, b2_ref)
    dl_slot = [0]

    def epilogue_a_q(col):
        zs = [slab(col)[...], slab(col + LANES)[...]]
        inv = inv_rms(zs, seg_ref[...])
        for i, z in enumerate(zs):
            zn = z * inv[:, i * LANES:(i + 1) * LANES] * qg_ref[...]
            c0 = col + i * LANES
            qa_ref[0, :, c0:c0 + LANES] = rope(zn, ta_ref, a_sh).astype(BF16)

    def epilogue_a_kv(col):
        zk = slab(col)[...]
        zk = rope(zk * inv_rms([zk], seg_ref[:LANES, :LANES]) * kg_ref[...], ta_ref, a_sh)
        zkr = pltpu.roll(zk, HEAD_DIM, 1)
        ka_ref[0, 0] = jnp.where(low, zk, zkr).astype(BF16)
        ka_ref[0, 1] = jnp.where(low, zkr, zk).astype(BF16)
        zv = slab(col + LANES)[...]
        va_ref[0, 0] = jnp.where(low, zv, 1.0).astype(BF16)
        va_ref[0, 1] = jnp.where(low, pltpu.roll(zv, HEAD_DIM, 1), 1.0).astype(BF16)

    def epilogue_b(col):
        t, rem = divmod(col - b_col, B_W)
        g, rem = divmod(rem, B_GROUP_W)
        d = B_DILATIONS[g]
        out_col = t * B_GROUP_W + rem
        tab_ref = (tbq_ref, tbk_ref, None)[t]
        src = slab(col)
        if tab_ref is not None:
            z = rope(src[...], tab_ref, b_sh)
            if d == 1:
                b_refs[g][0, 0, :, out_col:out_col + LANES] = z.astype(BF16)
                return
            src = dl_ref.at[dl_slot[0]]
            dl_slot[0] = 1 - dl_slot[0]
            src[...] = z
        elif d == 1:
            b_refs[g][0, 0, :, out_col:out_col + LANES] = src[...].astype(BF16)
            return
        for r in range(d):
            piece = src[pl.ds(r, tn // d, stride=d), :]
            b_refs[g][0, r, :, out_col:out_col + LANES] = piece.astype(BF16)

    def epilogue(c):
        col = c * PROJ_CHUNK
        while col < (c + 1) * PROJ_CHUNK:
            if col < kv_col:
                epilogue_a_q(col)
                col += 2 * LANES
            elif col < b_col:
                epilogue_a_kv(col)
                col += 2 * LANES
            elif col < g_col:
                epilogue_b(col)
                col += LANES
            else:
                gate = 1.0 / (1.0 + jnp.exp(-slab(col)[...]))
                gate_ref[0, :, col - g_col:col - g_col + LANES] = gate.astype(BF16)
                col += LANES

    stage(order[0])
    for pos, c in enumerate(order):
        if pos + 1 < n_chunk:
            stage(order[pos + 1])
        epilogue(c)


def _inproj(x, w_in, g_mix, q_gain, k_gain, tab_a, tab_bq, tab_bk, *, tn):
    bn, seq, dm = x.shape
    in_w = w_in.shape[1]
    gate_w = in_w - (A_Q_W + 2 * A_KV_W + 3 * B_W)
    grid = (bn, seq // tn)
    tab_spec = pl.BlockSpec((3, tn, LANES), lambda b, i: (0, i, 0))
    head_of = np.arange(2 * LANES) // HEAD_DIM
    seg = jnp.asarray(head_of[:, None] == head_of[None, :], BF16)
    out_shape = (
        jax.ShapeDtypeStruct((bn, seq, A_Q_W), BF16),
        jax.ShapeDtypeStruct((bn, A_KV_HEADS, seq, LANES), BF16),
        jax.ShapeDtypeStruct((bn, A_KV_HEADS, seq, LANES), BF16),
    ) + tuple(jax.ShapeDtypeStruct((bn, d, seq // d, 3 * B_GROUP_W), BF16) for d in B_DILATIONS) + (
        jax.ShapeDtypeStruct((bn, seq, gate_w), BF16),
    )
    out_specs = (
        pl.BlockSpec((1, tn, A_Q_W), lambda b, i: (b, i, 0)),
        pl.BlockSpec((1, A_KV_HEADS, tn, LANES), lambda b, i: (b, 0, i, 0)),
        pl.BlockSpec((1, A_KV_HEADS, tn, LANES), lambda b, i: (b, 0, i, 0)),
    ) + tuple(pl.BlockSpec((1, d, tn // d, 3 * B_GROUP_W), lambda b, i: (b, 0, i, 0))
              for d in B_DILATIONS) + (
        pl.BlockSpec((1, tn, gate_w), lambda b, i: (b, i, 0)),
    )
    return pl.pallas_call(
        functools.partial(_inproj_kernel, tn=tn),
        grid=grid,
        in_specs=[
            pl.BlockSpec((1, tn, dm), lambda b, i: (b, i, 0)),
            _const_spec((1, dm)),
            _const_spec((dm, in_w)),
            _const_spec((1, LANES)),
            _const_spec((1, LANES)),
            _const_spec(seg.shape),
            tab_spec, tab_spec, tab_spec,
        ],
        out_specs=out_specs,
        out_shape=out_shape,
        scratch_shapes=[pltpu.VMEM((2, PROJ_CHUNK // LANES, tn, LANES), F32),
                        pltpu.VMEM((2, tn, LANES), F32)],
        compiler_params=pltpu.CompilerParams(
            dimension_semantics=("parallel", "parallel"), vmem_limit_bytes=VMEM_LIMIT),
        name="inproj",
    )(x, g_mix, w_in, q_gain, k_gain, seg, tab_a, tab_bq, tab_bk)


def _attn_a_kernel(q_ref, k_ref, v_ref, o_ref, qs_ref, s_ref, pm_ref, m_ref, acc_ref,
                   *, tq, tk, nk):
    group = A_Q_HEADS // A_KV_HEADS
    n_lc = tk // LANES
    lane = lax.broadcasted_iota(jnp.int32, (tq, LANES), 1)
    low = lane < HEAD_DIM
    for h in range(group):
        pair = q_ref[0, :, (h // 2) * LANES:(h // 2 + 1) * LANES].astype(F32)
        keep = low if h % 2 == 0 else jnp.logical_not(low)
        qs_ref[h * tq:(h + 1) * tq, :] = jnp.where(keep, pair, 0.0).astype(BF16)
    m_ref[...] = jnp.full(m_ref.shape, MASK_VALUE, F32)
    acc_ref[...] = jnp.zeros(acc_ref.shape, F32)

    def scores(c):
        k = k_ref[0, 0, c * tk:(c + 1) * tk, :]
        s = lax.dot_general(qs_ref[...], k, (((1,), (1,)), ((), ())),
                            preferred_element_type=F32)
        s_ref[c % 2] = s
        pm = s[:, :LANES]
        for j in range(1, n_lc):
            pm = jnp.maximum(pm, s[:, j * LANES:(j + 1) * LANES])
        pm_ref[c % 2] = pm

    scores(0)
    for c in range(nk):
        if c + 1 < nk:
            scores(c + 1)
        slot = c % 2
        m_prev = m_ref[...]
        m_next = jnp.maximum(m_prev, jnp.max(pm_ref[slot], axis=1, keepdims=True))
        alpha = jnp.exp2(m_prev - m_next)
        m_ref[...] = m_next
        p = jnp.concatenate(
            [jnp.exp2(s_ref[slot, :, j * LANES:(j + 1) * LANES] - m_next) for j in range(n_lc)],
            axis=1).astype(BF16)
        v = v_ref[0, 0, c * tk:(c + 1) * tk, :]
        acc_ref[...] = alpha * acc_ref[...] + jnp.dot(p, v, preferred_element_type=F32)

    acc = acc_ref[...]
    acc_sw = pltpu.roll(acc, HEAD_DIM, 1)
    for pr in range(group // 2):
        ev = slice((2 * pr) * tq, (2 * pr + 1) * tq)
        od = slice((2 * pr + 1) * tq, (2 * pr + 2) * tq)
        out = jnp.where(low, acc[ev] / acc_sw[ev], acc_sw[od] / acc[od])
        o_ref[0, :, pr * LANES:(pr + 1) * LANES] = out.astype(BF16)


def _attn_a(qa, ka, va, *, tq, tk):
    bn, seq, _ = qa.shape
    group = A_Q_HEADS // A_KV_HEADS
    gw = group * HEAD_DIM
    grid = (bn, A_KV_HEADS, seq // tq)
    kv_spec = pl.BlockSpec((1, 1, seq, LANES), lambda b, j, i: (b, j, 0, 0))
    return pl.pallas_call(
        functools.partial(_attn_a_kernel, tq=tq, tk=tk, nk=seq // tk),
        grid=grid,
        in_specs=[pl.BlockSpec((1, tq, gw), lambda b, j, i: (b, i, j)), kv_spec, kv_spec],
        out_specs=pl.BlockSpec((1, tq, gw), lambda b, j, i: (b, i, j)),
        out_shape=jax.ShapeDtypeStruct((bn, seq, A_Q_W), BF16),
        scratch_shapes=[
            pltpu.VMEM((group * tq, LANES), BF16),
            pltpu.VMEM((2, group * tq, tk), F32),
            pltpu.VMEM((2, group * tq, LANES), F32),
            pltpu.VMEM((group * tq, LANES), F32),
            pltpu.VMEM((group * tq, LANES), F32),
        ],
        compiler_params=pltpu.CompilerParams(
            dimension_semantics=("parallel", "parallel", "parallel"), vmem_limit_bytes=VMEM_LIMIT),
        name="attn_a",
    )(qa, ka, va)


def _attn_b_kernel(q_ref, k_ref, v_ref, hmask_ref, bias_ref, o_ref, lse_ref,
                   *, sub_len, tq, tk, rb):
    nh = B_HEADS_PER_GROUP
    per_res = sub_len // tq
    n_tiles = rb * per_res
    low = lax.broadcasted_iota(jnp.int32, (tq, LANES), 1) < HEAD_DIM

    def locate(t):
        if isinstance(t, int):
            r, i = divmod(t, per_res)
            return r, i * tq, max(0, min(i * tq - B_HALF_WINDOW, sub_len - tk))
        r = t // per_res if rb > 1 else 0
        i0 = pl.multiple_of((t - r * per_res) * tq, tq)
        w0 = jnp.clip(i0 - B_HALF_WINDOW, 0, sub_len - tk)
        return r, i0, pl.multiple_of(w0, B_HALF_WINDOW)

    def tile(t):
        r, i0, w0 = locate(t)
        q = q_ref[0, r, pl.ds(i0, tq), :]
        k = k_ref[0, r, pl.ds(w0, tk), :]
        v = v_ref[0, r, pl.ds(w0, tk), :]
        qs = jnp.concatenate([q * hmask_ref[h] for h in range(nh)], axis=0)
        s = lax.dot_general(qs, k, (((1,), (1,)), ((), ())), preferred_element_type=F32)
        bias = bias_ref[(i0 - w0) // B_HALF_WINDOW]
        s = s + jnp.concatenate([bias] * nh, axis=0)
        m = jnp.max(s, axis=1, keepdims=True)
        p = jnp.exp2(s - m)
        l = jnp.broadcast_to(jnp.sum(p, axis=1, keepdims=True), (nh * tq, LANES))
        lse = jnp.broadcast_to(m, (nh * tq, LANES)) + jnp.log(l) * LOG2_E
        pv = jnp.dot(p.astype(BF16), v, preferred_element_type=F32)
        for half in range(B_GROUP_W // LANES):
            lanes = slice(half * LANES, (half + 1) * LANES)
            ev = slice((2 * half) * tq, (2 * half + 1) * tq)
            od = slice((2 * half + 1) * tq, (2 * half + 2) * tq)
            num = jnp.where(low, pv[ev, lanes], pv[od, lanes])
            den = jnp.where(low, l[ev], l[od])
            o_ref[0, r, pl.ds(i0, tq), lanes] = (num / den).astype(BF16)
            lse_ref[0, r, pl.ds(i0, tq), lanes] = jnp.where(low, lse[ev], lse[od])

    def group(g, carry):
        for u in range(B_INTERLEAVE):
            tile(g * B_INTERLEAVE + u)
        return carry

    lax.fori_loop(0, n_tiles // B_INTERLEAVE, group, 0)


def _attn_b(qkv, *, tq=B_Q_TILE, min_tiles=B_MIN_TILES):
    bn, d, sub_len, _ = qkv.shape
    tq = min(tq, sub_len)
    tk = min(tq + 2 * B_HALF_WINDOW, sub_len)
    per_res = sub_len // tq
    rb = min(d, max(1, min_tiles // per_res))
    assert (rb * per_res) % B_INTERLEAVE == 0
    lane_head = np.arange(B_GROUP_W) // HEAD_DIM
    hmask = lane_head[None, None, :] == np.arange(B_HEADS_PER_GROUP)[:, None, None]
    hmask = jnp.asarray(np.broadcast_to(hmask, (B_HEADS_PER_GROUP, tq, B_GROUP_W)), BF16)
    rel = np.arange(tq)[:, None] - np.arange(tk)[None, :]
    offs = np.arange(3)[:, None, None] * B_HALF_WINDOW
    bias = jnp.asarray(np.where(np.abs(rel[None] + offs) <= B_HALF_WINDOW, 0.0, MASK_VALUE), F32)

    def spec(c):
        return pl.BlockSpec((1, rb, sub_len, B_GROUP_W), lambda b, r: (b, r, 0, c))

    out_spec = pl.BlockSpec((1, rb, sub_len, B_GROUP_W), lambda b, r: (b, r, 0, 0))
    return pl.pallas_call(
        functools.partial(_attn_b_kernel, sub_len=sub_len, tq=tq, tk=tk, rb=rb),
        grid=(bn, d // rb),
        in_specs=[spec(0), spec(1), spec(2), _const_spec(hmask.shape), _const_spec(bias.shape)],
        out_specs=(out_spec, out_spec),
        out_shape=(jax.ShapeDtypeStruct((bn, d, sub_len, B_GROUP_W), BF16),
                   jax.ShapeDtypeStruct((bn, d, sub_len, B_GROUP_W), F32)),
        compiler_params=pltpu.CompilerParams(
            dimension_semantics=("parallel", "parallel"), vmem_limit_bytes=VMEM_LIMIT),
        name="attn_b",
    )(qkv, qkv, qkv, hmask, bias)


def _post_kernel(x_ref, ya_ref, o0_ref, l0_ref, o1_ref, l1_ref, o2_ref, l2_ref, gate_ref,
                 wa_ref, wb_ref, wo_ref, gmlp_ref, wup_ref, wdn_ref, gfin_ref,
                 y_ref, ril_ref, *, tn, ff_chunk, final_norm):
    dm = x_ref.shape[-1]
    n_slab = B_GROUP_W // LANES

    def to_natural_order(src_ref, d, slot):
        for r in range(d):
            blk = src_ref[0, r].astype(F32)
            for s in range(n_slab):
                ril_ref[slot * n_slab + s, pl.ds(r, tn // d, stride=d), :] = blk[:, s * LANES:(s + 1) * LANES]

    to_natural_order(o1_ref, B_DILATIONS[1], 0)
    to_natural_order(l1_ref, B_DILATIONS[1], 1)
    to_natural_order(o2_ref, B_DILATIONS[2], 2)
    to_natural_order(l2_ref, B_DILATIONS[2], 3)

    def natural(slot):
        return jnp.concatenate([ril_ref[slot * n_slab + s] for s in range(n_slab)], axis=1)

    o0 = o0_ref[0, 0].astype(F32)
    l0 = l0_ref[0, 0]
    o1, l1, o2, l2 = (natural(slot) for slot in range(4))
    mx = jnp.maximum(jnp.maximum(l0, l1), l2)
    e0, e1, e2 = jnp.exp2(l0 - mx), jnp.exp2(l1 - mx), jnp.exp2(l2 - mx)
    yb_in = (e0 * o0 + e1 * o1 + e2 * o2) / (e0 + e1 + e2)

    ya = jnp.dot(ya_ref[0], wa_ref[...], preferred_element_type=F32)
    yb = jnp.dot(yb_in.astype(BF16), wb_ref[...], preferred_element_type=F32)
    ga = gate_ref[0, :, :dm].astype(F32)
    gb = gate_ref[0, :, dm:].astype(F32)
    mixed = ya * ga + yb * gb
    x1 = x_ref[0] + jnp.dot(mixed.astype(BF16), wo_ref[...], preferred_element_type=F32)

    ms = jnp.mean(x1 * x1, axis=-1, keepdims=True)
    h2 = (x1 * lax.rsqrt(ms + EPS) * gmlp_ref[...]).astype(BF16)
    acc = x1
    d_ff = wup_ref.shape[1]
    for c in range(d_ff // ff_chunk):
        u = jnp.dot(h2, wup_ref[:, c * ff_chunk:(c + 1) * ff_chunk], preferred_element_type=F32)
        u = jnp.square(jnp.maximum(u, 0.0)).astype(BF16)
        acc = acc + jnp.dot(u, wdn_ref[c * ff_chunk:(c + 1) * ff_chunk, :],
                            preferred_element_type=F32)
    if final_norm:
        ms = jnp.mean(acc * acc, axis=-1, keepdims=True)
        acc = acc * lax.rsqrt(ms + EPS) * gfin_ref[...]
    y_ref[0] = acc


def _post(x, ya, b_outs, gates, w_a_out, w_b_out, w_out, g_mlp, w_up, w_down, g_final,
          *, tn, final_norm, ff_chunk=MLP_CHUNK):
    bn, seq, dm = x.shape
    d_ff = w_up.shape[1]
    grid = (bn, seq // tn)
    in_specs = [
        pl.BlockSpec((1, tn, dm), lambda b, i: (b, i, 0)),
        pl.BlockSpec((1, tn, A_Q_W), lambda b, i: (b, i, 0)),
    ]
    operands = [x, ya]
    for d, (o, lse) in zip(B_DILATIONS, b_outs):
        spec = pl.BlockSpec((1, d, tn // d, B_GROUP_W), lambda b, i: (b, 0, i, 0))
        in_specs += [spec, spec]
        operands += [o, lse]
    in_specs += [
        pl.BlockSpec((1, tn, gates.shape[-1]), lambda b, i: (b, i, 0)),
        _const_spec(w_a_out.shape), _const_spec(w_b_out.shape), _const_spec(w_out.shape),
        _const_spec((1, dm)), _const_spec(w_up.shape), _const_spec(w_down.shape), _const_spec((1, dm)),
    ]
    operands += [gates, w_a_out, w_b_out, w_out, g_mlp, w_up, w_down, g_final]
    n_ril = 4 * (B_GROUP_W // LANES)
    return pl.pallas_call(
        functools.partial(_post_kernel, tn=tn, ff_chunk=ff_chunk, final_norm=final_norm),
        grid=grid,
        in_specs=in_specs,
        out_specs=pl.BlockSpec((1, tn, dm), lambda b, i: (b, i, 0)),
        out_shape=jax.ShapeDtypeStruct((bn, seq, dm), F32),
        scratch_shapes=[pltpu.VMEM((n_ril, tn, LANES), F32)],
        compiler_params=pltpu.CompilerParams(
            dimension_semantics=("parallel", "parallel"), vmem_limit_bytes=VMEM_LIMIT),
        name="post",
    )(*operands)


def _layer(x, w, tabs, *, final_norm, g_final):
    tab_a, tab_bq, tab_bk = tabs
    seq = x.shape[1]
    assert seq % TOKEN_TILE == 0 and seq % A_Q_TILE == 0 and seq % A_K_TILE == 0
    assert TOKEN_TILE % (BF16_ROWS * max(B_DILATIONS)) == 0
    qa, ka, va, b0, b1, b2, gates = _inproj(
        x, w["w_in"], w["g_mix"], w["q_gain"], w["k_gain"], tab_a, tab_bq, tab_bk, tn=TOKEN_TILE)
    ya = _attn_a(qa, ka, va, tq=A_Q_TILE, tk=A_K_TILE)
    b_outs = [_attn_b(b) for b in (b0, b1, b2)]
    return _post(x, ya, b_outs, gates, w["w_a_out"], w["w_b_out"], w["w_out"], w["g_mlp"],
                 w["w_up"], w["w_down"], g_final, tn=TOKEN_TILE, final_norm=final_norm)


def _trunk(x, layers, g_final):
    seq = x.shape[1]
    tab_a, tab_b = _rope_tables(seq)
    tabs = tuple(jnp.asarray(t, F32) for t in (tab_a, tab_b * Q_SCALE, tab_b))
    for i, w in enumerate(layers):
        x = _layer(x, w, tabs, final_norm=(i == len(layers) - 1), g_final=g_final)
    return x


def kernel(x_prompt, x_sample, w_in, w_a_out, w_b_out, w_out, g_mix, q_gain, k_gain, g_mlp,
           w_up, w_down, g_final):
    depth = w_in.shape[0]
    reps = LANES // HEAD_DIM
    layers = []
    for l in range(depth):
        layers.append(dict(
            w_in=w_in[l].astype(BF16), w_a_out=w_a_out[l].astype(BF16),
            w_b_out=w_b_out[l].astype(BF16), w_out=w_out[l].astype(BF16),
            w_up=w_up[l].astype(BF16), w_down=w_down[l].astype(BF16),
            g_mix=g_mix[l][None, :], g_mlp=g_mlp[l][None, :],
            q_gain=jnp.tile(q_gain[l] * Q_SCALE, reps)[None, :],
            k_gain=jnp.tile(k_gain[l], reps)[None, :],
        ))
    gf = g_final[None, :]
    return (_trunk(x_prompt, layers, gf), _trunk(x_sample, layers, gf))
```

```python
import functools
import math

import jax
import jax.numpy as jnp
import numpy as np
from jax import lax
from jax.experimental import pallas as pl
from jax.experimental.pallas import tpu as pltpu

F32 = jnp.float32
BF16 = jnp.bfloat16

HEAD_DIM = 64
A_Q_HEADS = 8
A_KV_HEADS = 2
B_DILATIONS = (1, 4, 16)
B_HALF_WINDOW = 64
B_HEADS_PER_GROUP = 4
GRID_W = 64
AXIAL_THETA = 10000.0
PARTIAL_THETA = 500000.0
PARTIAL_ROPE_DIM = HEAD_DIM // 4
EPS = 1e-6
MASK_VALUE = -1e30

LANES = 128
BF16_ROWS = 16
A_Q_W = A_Q_HEADS * HEAD_DIM
A_KV_W = A_KV_HEADS * HEAD_DIM
B_GROUP_W = B_HEADS_PER_GROUP * HEAD_DIM
B_W = len(B_DILATIONS) * B_GROUP_W
LOG2_E = math.log2(math.e)
Q_SCALE = HEAD_DIM ** -0.5 * LOG2_E

TOKEN_TILE = 512
PROJ_CHUNK = 512
MLP_CHUNK = 512
A_Q_TILE = 1024
A_K_TILE = 1024
B_Q_TILE = 128
B_MIN_TILES = 32
B_INTERLEAVE = 32
VMEM_LIMIT = 56 * 1024 * 1024


def _const_spec(shape):
    return pl.BlockSpec(shape, lambda *_: (0,) * len(shape), pipeline_mode=pl.Buffered(1))


def _rope_tables(seq):
    t = np.arange(seq)

    def half_tables(pos, d, theta):
        d2 = d // 2
        freqs = theta ** (-(np.arange(d2, dtype=np.float64) * 2.0) / d)
        ang = pos.astype(np.float64)[:, None] * freqs[None, :]
        cos, sin = np.cos(ang), np.sin(ang)
        zero = np.zeros_like(sin)
        return (np.concatenate([cos, cos], -1), np.concatenate([-sin, zero], -1),
                np.concatenate([zero, sin], -1))

    half = HEAD_DIM // 2
    row = half_tables(t // GRID_W, half, AXIAL_THETA)
    col = half_tables(t % GRID_W, half, AXIAL_THETA)
    axial = [np.concatenate([r, c], -1) for r, c in zip(row, col)]
    part = half_tables(t, PARTIAL_ROPE_DIM, PARTIAL_THETA)
    rest = HEAD_DIM - PARTIAL_ROPE_DIM
    fill = (np.ones((seq, rest)), np.zeros((seq, rest)), np.zeros((seq, rest)))
    partial = [np.concatenate([p, f], -1) for p, f in zip(part, fill)]
    reps = LANES // HEAD_DIM
    tab_a = np.stack([np.tile(a, (1, reps)) for a in axial])
    tab_b = np.stack([np.tile(p, (1, reps)) for p in partial])
    return tab_a, tab_b


def _inproj_kernel(x_ref, gmix_ref, w_ref, qg_ref, kg_ref, seg_ref, ta_ref, tbq_ref, tbk_ref,
                   qa_ref, ka_ref, va_ref, b0_ref, b1_ref, b2_ref, gate_ref, z_ref, dl_ref, *, tn):
    x = x_ref[0]
    ms = jnp.mean(x * x, axis=-1, keepdims=True)
    h = (x * lax.rsqrt(ms + EPS) * gmix_ref[...]).astype(BF16)
    low = lax.broadcasted_iota(jnp.int32, (tn, LANES), 1) < HEAD_DIM
    n_slab = PROJ_CHUNK // LANES
    n_chunk = w_ref.shape[1] // PROJ_CHUNK
    first_gate = (A_Q_W + 2 * A_KV_W + 3 * B_W) // PROJ_CHUNK
    order = list(range(first_gate, n_chunk)) + list(range(first_gate))
    slot_of = {c: pos % 2 for pos, c in enumerate(order)}

    def stage(c):
        z = jnp.dot(h, w_ref[:, c * PROJ_CHUNK:(c + 1) * PROJ_CHUNK], preferred_element_type=F32)
        for s in range(n_slab):
            z_ref[slot_of[c], s] = z[:, s * LANES:(s + 1) * LANES]

    def slab(col):
        c, rem = divmod(col, PROJ_CHUNK)
        return z_ref.at[slot_of[c], rem // LANES]

    def inv_rms(zs, seg):
        z2 = jnp.concatenate([z * z for z in zs], axis=1).astype(BF16)
        ss = jnp.dot(z2, seg, preferred_element_type=F32)
        return lax.rsqrt(ss * (1.0 / HEAD_DIM) + EPS)

    def rope(z, tab_ref, sh):
        return (z * tab_ref[0] + pltpu.roll(z, LANES - sh, 1) * tab_ref[1]
                + pltpu.roll(z, sh, 1) * tab_ref[2])

    a_sh = HEAD_DIM // 4
    b_sh = PARTIAL_ROPE_DIM // 2
    kv_col = A_Q_W
    b_col = A_Q_W + 2 * A_KV_W
    g_col = b_col + 3 * B_W
    b_refs = (b0_ref, b1_ref, b2_ref)
    dl_slot = [0]

    def epilogue_a_q(col):
        zs = [slab(col)[...], slab(col + LANES)[...]]
        inv = inv_rms(zs, seg_ref[...])
        for i, z in enumerate(zs):
            zn = z * inv[:, i * LANES:(i + 1) * LANES] * qg_ref[...]
            c0 = col + i * LANES
            qa_ref[0, :, c0:c0 + LANES] = rope(zn, ta_ref, a_sh).astype(BF16)

    def epilogue_a_kv(col):
        zk = slab(col)[...]
        zk = rope(zk * inv_rms([zk], seg_ref[:LANES, :LANES]) * kg_ref[...], ta_ref, a_sh)
        zkr = pltpu.roll(zk, HEAD_DIM, 1)
        ka_ref[0, 0] = jnp.where(low, zk, zkr).astype(BF16)
        ka_ref[0, 1] = jnp.where(low, zkr, zk).astype(BF16)
        zv = slab(col + LANES)[...]
        va_ref[0, 0] = jnp.where(low, zv, 1.0).astype(BF16)
        va_ref[0, 1] = jnp.where(low, pltpu.roll(zv, HEAD_DIM, 1), 1.0).astype(BF16)

    def epilogue_b(col):
        t, rem = divmod(col - b_col, B_W)
        g, rem = divmod(rem, B_GROUP_W)
        d = B_DILATIONS[g]
        out_col = t * B_GROUP_W + rem
        tab_ref = (tbq_ref, tbk_ref, None)[t]
        src = slab(col)
        if tab_ref is not None:
            z = rope(src[...], tab_ref, b_sh)
            if d == 1:
                b_refs[g][0, 0, :, out_col:out_col + LANES] = z.astype(BF16)
                return
            src = dl_ref.at[dl_slot[0]]
            dl_slot[0] = 1 - dl_slot[0]
            src[...] = z
        elif d == 1:
            b_refs[g][0, 0, :, out_col:out_col + LANES] = src[...].astype(BF16)
            return
        for r in range(d):
            piece = src[pl.ds(r, tn // d, stride=d), :]
            b_refs[g][0, r, :, out_col:out_col + LANES] = piece.astype(BF16)

    def epilogue(c):
        col = c * PROJ_CHUNK
        while col < (c + 1) * PROJ_CHUNK:
            if col < kv_col:
                epilogue_a_q(col)
                col += 2 * LANES
            elif col < b_col:
                epilogue_a_kv(col)
                col += 2 * LANES
            elif col < g_col:
                epilogue_b(col)
                col += LANES
            else:
                gate = 1.0 / (1.0 + jnp.exp(-slab(col)[...]))
                gate_ref[0, :, col - g_col:col - g_col + LANES] = gate.astype(BF16)
                col += LANES

    stage(order[0])
    for pos, c in enumerate(order):
        if pos + 1 < n_chunk:
            stage(order[pos + 1])
        epilogue(c)


def _inproj(x, w_in, g_mix, q_gain, k_gain, tab_a, tab_bq, tab_bk, *, tn):
    bn, seq, dm = x.shape
    in_w = w_in.shape[1]
    gate_w = in_w - (A_Q_W + 2 * A_KV_W + 3 * B_W)
    grid = (bn, seq // tn)
    tab_spec = pl.BlockSpec((3, tn, LANES), lambda b, i: (0, i, 0))
    head_of = np.arange(2 * LANES) // HEAD_DIM
    seg = jnp.asarray(head_of[:, None] == head_of[None, :], BF16)
    out_shape = (
        jax.ShapeDtypeStruct((bn, seq, A_Q_W), BF16),
        jax.ShapeDtypeStruct((bn, A_KV_HEADS, seq, LANES), BF16),
        jax.ShapeDtypeStruct((bn, A_KV_HEADS, seq, LANES), BF16),
    ) + tuple(jax.ShapeDtypeStruct((bn, d, seq // d, 3 * B_GROUP_W), BF16) for d in B_DILATIONS) + (
        jax.ShapeDtypeStruct((bn, seq, gate_w), BF16),
    )
    out_specs = (
        pl.BlockSpec((1, tn, A_Q_W), lambda b, i: (b, i, 0)),
        pl.BlockSpec((1, A_KV_HEADS, tn, LANES), lambda b, i: (b, 0, i, 0)),
        pl.BlockSpec((1, A_KV_HEADS, tn, LANES), lambda b, i: (b, 0, i, 0)),
    ) + tuple(pl.BlockSpec((1, d, tn // d, 3 * B_GROUP_W), lambda b, i: (b, 0, i, 0))
              for d in B_DILATIONS) + (
        pl.BlockSpec((1, tn, gate_w), lambda b, i: (b, i, 0)),
    )
    return pl.pallas_call(
        functools.partial(_inproj_kernel, tn=tn),
        grid=grid,
        in_specs=[
            pl.BlockSpec((1, tn, dm), lambda b, i: (b, i, 0)),
            _const_spec((1, dm)),
            _const_spec((dm, in_w)),
            _const_spec((1, LANES)),
            _const_spec((1, LANES)),
            _const_spec(seg.shape),
            tab_spec, tab_spec, tab_spec,
        ],
        out_specs=out_specs,
        out_shape=out_shape,
        scratch_shapes=[pltpu.VMEM((2, PROJ_CHUNK // LANES, tn, LANES), F32),
                        pltpu.VMEM((2, tn, LANES), F32)],
        compiler_params=pltpu.CompilerParams(
            dimension_semantics=("parallel", "parallel"), vmem_limit_bytes=VMEM_LIMIT),
        name="inproj",
    )(x, g_mix, w_in, q_gain, k_gain, seg, tab_a, tab_bq, tab_bk)


def _attn_a_kernel(q_ref, k_ref, v_ref, o_ref, qs_ref, s_ref, pm_ref, m_ref, acc_ref,
                   *, tq, tk, nk):
    group = A_Q_HEADS // A_KV_HEADS
    n_lc = tk // LANES
    lane = lax.broadcasted_iota(jnp.int32, (tq, LANES), 1)
    low = lane < HEAD_DIM
    for h in range(group):
        pair = q_ref[0, :, (h // 2) * LANES:(h // 2 + 1) * LANES].astype(F32)
        keep = low if h % 2 == 0 else jnp.logical_not(low)
        qs_ref[h * tq:(h + 1) * tq, :] = jnp.where(keep, pair, 0.0).astype(BF16)
    m_ref[...] = jnp.full(m_ref.shape, MASK_VALUE, F32)
    acc_ref[...] = jnp.zeros(acc_ref.shape, F32)

    def scores(c):
        k = k_ref[0, 0, c * tk:(c + 1) * tk, :]
        s = lax.dot_general(qs_ref[...], k, (((1,), (1,)), ((), ())),
                            preferred_element_type=F32)
        s_ref[c % 2] = s
        pm = s[:, :LANES]
        for j in range(1, n_lc):
            pm = jnp.maximum(pm, s[:, j * LANES:(j + 1) * LANES])
        pm_ref[c % 2] = pm

    scores(0)
    for c in range(nk):
        if c + 1 < nk:
            scores(c + 1)
        slot = c % 2
        m_prev = m_ref[...]
        m_next = jnp.maximum(m_prev, jnp.max(pm_ref[slot], axis=1, keepdims=True))
        alpha = jnp.exp2(m_prev - m_next)
        m_ref[...] = m_next
        p = jnp.concatenate(
            [jnp.exp2(s_ref[slot, :, j * LANES:(j + 1) * LANES] - m_next) for j in range(n_lc)],
            axis=1).astype(BF16)
        v = v_ref[0, 0, c * tk:(c + 1) * tk, :]
        acc_ref[...] = alpha * acc_ref[...] + jnp.dot(p, v, preferred_element_type=F32)

    acc = acc_ref[...]
    acc_sw = pltpu.roll(acc, HEAD_DIM, 1)
    for pr in range(group // 2):
        ev = slice((2 * pr) * tq, (2 * pr + 1) * tq)
        od = slice((2 * pr + 1) * tq, (2 * pr + 2) * tq)
        out = jnp.where(low, acc[ev] / acc_sw[ev], acc_sw[od] / acc[od])
        o_ref[0, :, pr * LANES:(pr + 1) * LANES] = out.astype(BF16)


def _attn_a(qa, ka, va, *, tq, tk):
    bn, seq, _ = qa.shape
    group = A_Q_HEADS // A_KV_HEADS
    gw = group * HEAD_DIM
    grid = (bn, A_KV_HEADS, seq // tq)
    kv_spec = pl.BlockSpec((1, 1, seq, LANES), lambda b, j, i: (b, j, 0, 0))
    return pl.pallas_call(
        functools.partial(_attn_a_kernel, tq=tq, tk=tk, nk=seq // tk),
        grid=grid,
        in_specs=[pl.BlockSpec((1, tq, gw), lambda b, j, i: (b, i, j)), kv_spec, kv_spec],
        out_specs=pl.BlockSpec((1, tq, gw), lambda b, j, i: (b, i, j)),
        out_shape=jax.ShapeDtypeStruct((bn, seq, A_Q_W), BF16),
        scratch_shapes=[
            pltpu.VMEM((group * tq, LANES), BF16),
            pltpu.VMEM((2, group * tq, tk), F32),
            pltpu.VMEM((2, group * tq, LANES), F32),
            pltpu.VMEM((group * tq, LANES), F32),
            pltpu.VMEM((group * tq, LANES), F32),
        ],
        compiler_params=pltpu.CompilerParams(
            dimension_semantics=("parallel", "parallel", "parallel"), vmem_limit_bytes=VMEM_LIMIT),
        name="attn_a",
    )(qa, ka, va)


def _attn_b_kernel(q_ref, k_ref, v_ref, hmask_ref, bias_ref, o_ref, lse_ref,
                   *, sub_len, tq, tk, rb, bb):
    nh = B_HEADS_PER_GROUP
    per_res = sub_len // tq
    n_tiles = bb * rb * per_res
    low = lax.broadcasted_iota(jnp.int32, (tq, LANES), 1) < HEAD_DIM

    def locate(t):
        if isinstance(t, int):
            br, i = divmod(t, per_res)
            b, r = divmod(br, rb)
            return b, r, i * tq, max(0, min(i * tq - B_HALF_WINDOW, sub_len - tk))
        br = t // per_res
        b = br // rb
        i0 = pl.multiple_of((t - br * per_res) * tq, tq)
        w0 = jnp.clip(i0 - B_HALF_WINDOW, 0, sub_len - tk)
        return b, br - b * rb, i0, pl.multiple_of(w0, B_HALF_WINDOW)

    def tile(t):
        b, r, i0, w0 = locate(t)
        q = q_ref[b, r, pl.ds(i0, tq), :]
        k = k_ref[b, r, pl.ds(w0, tk), :]
        v = v_ref[b, r, pl.ds(w0, tk), :]
        qs = jnp.concatenate([q * hmask_ref[h] for h in range(nh)], axis=0)
        s = lax.dot_general(qs, k, (((1,), (1,)), ((), ())), preferred_element_type=F32)
        bias = bias_ref[(i0 - w0) // B_HALF_WINDOW]
        s = s + jnp.concatenate([bias] * nh, axis=0)
        m = jnp.max(s, axis=1, keepdims=True)
        p = jnp.exp2(s - m)
        l = jnp.broadcast_to(jnp.sum(p, axis=1, keepdims=True), (nh * tq, LANES))
        lse = jnp.broadcast_to(m, (nh * tq, LANES)) + jnp.log(l) * LOG2_E
        pv = jnp.dot(p.astype(BF16), v, preferred_element_type=F32)
        for half in range(B_GROUP_W // LANES):
            lanes = slice(half * LANES, (half + 1) * LANES)
            ev = slice((2 * half) * tq, (2 * half + 1) * tq)
            od = slice((2 * half + 1) * tq, (2 * half + 2) * tq)
            num = jnp.where(low, pv[ev, lanes], pv[od, lanes])
            den = jnp.where(low, l[ev], l[od])
            o_ref[b, r, pl.ds(i0, tq), lanes] = (num / den).astype(BF16)
            lse_ref[b, r, pl.ds(i0, tq), lanes] = jnp.where(low, lse[ev], lse[od])

    il = min(B_INTERLEAVE, n_tiles)
    if il == n_tiles:
        for t in range(n_tiles):
            tile(t)
    else:
        def group(g, carry):
            for u in range(il):
                tile(g * il + u)
            return carry

        lax.fori_loop(0, n_tiles // il, group, 0)


def _attn_b(qkv, *, tq=B_Q_TILE, min_tiles=B_MIN_TILES):
    bn, d, sub_len, _ = qkv.shape
    tq = min(tq, sub_len)
    tk = min(tq + 2 * B_HALF_WINDOW, sub_len)
    per_res = sub_len // tq
    rb = min(d, max(1, min_tiles // per_res))
    bb = min(bn, max(1, min_tiles // (rb * per_res)))
    n_tiles = bb * rb * per_res
    assert bn % bb == 0 and d % rb == 0 and n_tiles % min(B_INTERLEAVE, n_tiles) == 0
    lane_head = np.arange(B_GROUP_W) // HEAD_DIM
    hmask = lane_head[None, None, :] == np.arange(B_HEADS_PER_GROUP)[:, None, None]
    hmask = jnp.asarray(np.broadcast_to(hmask, (B_HEADS_PER_GROUP, tq, B_GROUP_W)), BF16)
    rel = np.arange(tq)[:, None] - np.arange(tk)[None, :]
    offs = np.arange(3)[:, None, None] * B_HALF_WINDOW
    bias = jnp.asarray(np.where(np.abs(rel[None] + offs) <= B_HALF_WINDOW, 0.0, MASK_VALUE), F32)

    def spec(c):
        return pl.BlockSpec((bb, rb, sub_len, B_GROUP_W), lambda b, r: (b, r, 0, c))

    out_spec = pl.BlockSpec((bb, rb, sub_len, B_GROUP_W), lambda b, r: (b, r, 0, 0))
    return pl.pallas_call(
        functools.partial(_attn_b_kernel, sub_len=sub_len, tq=tq, tk=tk, rb=rb, bb=bb),
        grid=(bn // bb, d // rb),
        in_specs=[spec(0), spec(1), spec(2), _const_spec(hmask.shape), _const_spec(bias.shape)],
        out_specs=(out_spec, out_spec),
        out_shape=(jax.ShapeDtypeStruct((bn, d, sub_len, B_GROUP_W), BF16),
                   jax.ShapeDtypeStruct((bn, d, sub_len, B_GROUP_W), F32)),
        compiler_params=pltpu.CompilerParams(
            dimension_semantics=("parallel", "parallel"), vmem_limit_bytes=VMEM_LIMIT),
        name="attn_b",
    )(qkv, qkv, qkv, hmask, bias)


def _post_kernel(x_ref, ya_ref, o0_ref, l0_ref, o1_ref, l1_ref, o2_ref, l2_ref, gate_ref,
                 wa_ref, wb_ref, wo_ref, gmlp_ref, wup_ref, wdn_ref, gfin_ref,
                 y_ref, ril_ref, *, tn, ff_chunk, final_norm):
    dm = x_ref.shape[-1]
    n_slab = B_GROUP_W // LANES

    def to_natural_order(src_ref, d, slot):
        for r in range(d):
            blk = src_ref[0, r].astype(F32)
            for s in range(n_slab):
                ril_ref[slot * n_slab + s, pl.ds(r, tn // d, stride=d), :] = blk[:, s * LANES:(s + 1) * LANES]

    to_natural_order(o1_ref, B_DILATIONS[1], 0)
    to_natural_order(l1_ref, B_DILATIONS[1], 1)
    to_natural_order(o2_ref, B_DILATIONS[2], 2)
    to_natural_order(l2_ref, B_DILATIONS[2], 3)

    def natural(slot):
        return jnp.concatenate([ril_ref[slot * n_slab + s] for s in range(n_slab)], axis=1)

    o0 = o0_ref[0, 0].astype(F32)
    l0 = l0_ref[0, 0]
    o1, l1, o2, l2 = (natural(slot) for slot in range(4))
    mx = jnp.maximum(jnp.maximum(l0, l1), l2)
    e0, e1, e2 = jnp.exp2(l0 - mx), jnp.exp2(l1 - mx), jnp.exp2(l2 - mx)
    yb_in = (e0 * o0 + e1 * o1 + e2 * o2) / (e0 + e1 + e2)

    ya = jnp.dot(ya_ref[0], wa_ref[...], preferred_element_type=F32)
    yb = jnp.dot(yb_in.astype(BF16), wb_ref[...], preferred_element_type=F32)
    ga = gate_ref[0, :, :dm].astype(F32)
    gb = gate_ref[0, :, dm:].astype(F32)
    mixed = ya * ga + yb * gb
    x1 = x_ref[0] + jnp.dot(mixed.astype(BF16), wo_ref[...], preferred_element_type=F32)

    ms = jnp.mean(x1 * x1, axis=-1, keepdims=True)
    h2 = (x1 * lax.rsqrt(ms + EPS) * gmlp_ref[...]).astype(BF16)
    acc = x1
    d_ff = wup_ref.shape[1]
    for c in range(d_ff // ff_chunk):
        u = jnp.dot(h2, wup_ref[:, c * ff_chunk:(c + 1) * ff_chunk], preferred_element_type=F32)
        u = jnp.square(jnp.maximum(u, 0.0)).astype(BF16)
        acc = acc + jnp.dot(u, wdn_ref[c * ff_chunk:(c + 1) * ff_chunk, :],
                            preferred_element_type=F32)
    if final_norm:
        ms = jnp.mean(acc * acc, axis=-1, keepdims=True)
        acc = acc * lax.rsqrt(ms + EPS) * gfin_ref[...]
    y_ref[0] = acc


def _post(x, ya, b_outs, gates, w_a_out, w_b_out, w_out, g_mlp, w_up, w_down, g_final,
          *, tn, final_norm, ff_chunk=MLP_CHUNK):
    bn, seq, dm = x.shape
    d_ff = w_up.shape[1]
    grid = (bn, seq // tn)
    in_specs = [
        pl.BlockSpec((1, tn, dm), lambda b, i: (b, i, 0)),
        pl.BlockSpec((1, tn, A_Q_W), lambda b, i: (b, i, 0)),
    ]
    operands = [x, ya]
    for d, (o, lse) in zip(B_DILATIONS, b_outs):
        spec = pl.BlockSpec((1, d, tn // d, B_GROUP_W), lambda b, i: (b, 0, i, 0))
        in_specs += [spec, spec]
        operands += [o, lse]
    in_specs += [
        pl.BlockSpec((1, tn, gates.shape[-1]), lambda b, i: (b, i, 0)),
        _const_spec(w_a_out.shape), _const_spec(w_b_out.shape), _const_spec(w_out.shape),
        _const_spec((1, dm)), _const_spec(w_up.shape), _const_spec(w_down.shape), _const_spec((1, dm)),
    ]
    operands += [gates, w_a_out, w_b_out, w_out, g_mlp, w_up, w_down, g_final]
    n_ril = 4 * (B_GROUP_W // LANES)
    return pl.pallas_call(
        functools.partial(_post_kernel, tn=tn, ff_chunk=ff_chunk, final_norm=final_norm),
        grid=grid,
        in_specs=in_specs,
        out_specs=pl.BlockSpec((1, tn, dm), lambda b, i: (b, i, 0)),
        out_shape=jax.ShapeDtypeStruct((bn, seq, dm), F32),
        scratch_shapes=[pltpu.VMEM((n_ril, tn, LANES), F32)],
        compiler_params=pltpu.CompilerParams(
            dimension_semantics=("parallel", "parallel"), vmem_limit_bytes=VMEM_LIMIT),
        name="post",
    )(*operands)


def _layer(x, w, tabs, *, final_norm, g_final):
    tab_a, tab_bq, tab_bk = tabs
    seq = x.shape[1]
    assert seq % TOKEN_TILE == 0 and seq % A_Q_TILE == 0 and seq % A_K_TILE == 0
    assert TOKEN_TILE % (BF16_ROWS * max(B_DILATIONS)) == 0
    qa, ka, va, b0, b1, b2, gates = _inproj(
        x, w["w_in"], w["g_mix"], w["q_gain"], w["k_gain"], tab_a, tab_bq, tab_bk, tn=TOKEN_TILE)
    ya = _attn_a(qa, ka, va, tq=A_Q_TILE, tk=A_K_TILE)
    b_outs = [_attn_b(b) for b in (b0, b1, b2)]
    return _post(x, ya, b_outs, gates, w["w_a_out"], w["w_b_out"], w["w_out"], w["g_mlp"],
                 w["w_up"], w["w_down"], g_final, tn=TOKEN_TILE, final_norm=final_norm)


def _trunk(x, layers, g_final):
    seq = x.shape[1]
    tab_a, tab_b = _rope_tables(seq)
    tabs = tuple(jnp.asarray(t, F32) for t in (tab_a, tab_b * Q_SCALE, tab_b))
    for i, w in enumerate(layers):
        x = _layer(x, w, tabs, final_norm=(i == len(layers) - 1), g_final=g_final)
    return x


def kernel(x_prompt, x_sample, w_in, w_a_out, w_b_out, w_out, g_mix, q_gain, k_gain, g_mlp,
           w_up, w_down, g_final):
    depth = w_in.shape[0]
    reps = LANES // HEAD_DIM
    layers = []
    for l in range(depth):
        layers.append(dict(
            w_in=w_in[l].astype(BF16), w_a_out=w_a_out[l].astype(BF16),
            w_b_out=w_b_out[l].astype(BF16), w_out=w_out[l].astype(BF16),
            w_up=w_up[l].astype(BF16), w_down=w_down[l].astype(BF16),
            g_mix=g_mix[l][None, :], g_mlp=g_mlp[l][None, :],
            q_gain=jnp.tile(q_gain[l] * Q_SCALE, reps)[None, :],
            k_gain=jnp.tile(k_gain[l], reps)[None, :],
        ))
    gf = g_final[None, :]
    return (_trunk(x_prompt, layers, gf), _trunk(x_sample, layers, gf))
```

```python
import functools
import math

import jax
import jax.numpy as jnp
import numpy as np
from jax import lax
from jax.experimental import pallas as pl
from jax.experimental.pallas import tpu as pltpu

F32 = jnp.float32
BF16 = jnp.bfloat16

HEAD_DIM = 64
A_Q_HEADS = 8
A_KV_HEADS = 2
B_DILATIONS = (1, 4, 16)
B_HALF_WINDOW = 64
B_HEADS_PER_GROUP = 4
GRID_W = 64
AXIAL_THETA = 10000.0
PARTIAL_THETA = 500000.0
PARTIAL_ROPE_DIM = HEAD_DIM // 4
EPS = 1e-6
MASK_VALUE = -1e30

LANES = 128
BF16_ROWS = 16
A_Q_W = A_Q_HEADS * HEAD_DIM
A_KV_W = A_KV_HEADS * HEAD_DIM
B_GROUP_W = B_HEADS_PER_GROUP * HEAD_DIM
B_W = len(B_DILATIONS) * B_GROUP_W
LOG2_E = math.log2(math.e)
Q_SCALE = HEAD_DIM ** -0.5 * LOG2_E

TOKEN_TILE = 512
PROJ_CHUNK = 512
MLP_CHUNK = 512
A_Q_TILE = 1024
A_K_TILE = 1024
B_Q_TILE = 128
B_MIN_TILES = 32
B_INTERLEAVE = 32
VMEM_LIMIT = 56 * 1024 * 1024


def _const_spec(shape):
    return pl.BlockSpec(shape, lambda *_: (0,) * len(shape), pipeline_mode=pl.Buffered(1))


def _rope_tables(seq):
    t = np.arange(seq)

    def half_tables(pos, d, theta):
        d2 = d // 2
        freqs = theta ** (-(np.arange(d2, dtype=np.float64) * 2.0) / d)
        ang = pos.astype(np.float64)[:, None] * freqs[None, :]
        cos, sin = np.cos(ang), np.sin(ang)
        zero = np.zeros_like(sin)
        return (np.concatenate([cos, cos], -1), np.concatenate([-sin, zero], -1),
                np.concatenate([zero, sin], -1))

    half = HEAD_DIM // 2
    row = half_tables(t // GRID_W, half, AXIAL_THETA)
    col = half_tables(t % GRID_W, half, AXIAL_THETA)
    axial = [np.concatenate([r, c], -1) for r, c in zip(row, col)]
    part = half_tables(t, PARTIAL_ROPE_DIM, PARTIAL_THETA)
    rest = HEAD_DIM - PARTIAL_ROPE_DIM
    fill = (np.ones((seq, rest)), np.zeros((seq, rest)), np.zeros((seq, rest)))
    partial = [np.concatenate([p, f], -1) for p, f in zip(part, fill)]
    reps = LANES // HEAD_DIM
    tab_a = np.stack([np.tile(a, (1, reps)) for a in axial])
    tab_b = np.stack([np.tile(p, (1, reps)) for p in partial])
    return tab_a, tab_b


def _inproj_kernel(x_ref, gmix_ref, w_ref, qg_ref, kg_ref, seg_ref, ta_ref, tbq_ref, tbk_ref,
                   qa_ref, kv_ref, b0_ref, b1_ref, b2_ref, gate_ref, z_ref, dl_ref, *, tn):
    ka_ref = kv_ref.at[:, :A_KV_HEADS]
    va_ref = kv_ref.at[:, A_KV_HEADS:]
    x = x_ref[0]
    ms = jnp.mean(x * x, axis=-1, keepdims=True)
    h = (x * lax.rsqrt(ms + EPS) * gmix_ref[...]).astype(BF16)
    low = lax.broadcasted_iota(jnp.int32, (tn, LANES), 1) < HEAD_DIM
    n_slab = PROJ_CHUNK // LANES
    n_chunk = w_ref.shape[1] // PROJ_CHUNK
    first_gate = (A_Q_W + 2 * A_KV_W + 3 * B_W) // PROJ_CHUNK
    order = list(range(first_gate, n_chunk)) + list(range(first_gate))
    slot_of = {c: pos % 2 for pos, c in enumerate(order)}

    def stage(c):
        z = jnp.dot(h, w_ref[:, c * PROJ_CHUNK:(c + 1) * PROJ_CHUNK], preferred_element_type=F32)
        for s in range(n_slab):
            z_ref[slot_of[c], s] = z[:, s * LANES:(s + 1) * LANES]

    def slab(col):
        c, rem = divmod(col, PROJ_CHUNK)
        return z_ref.at[slot_of[c], rem // LANES]

    def inv_rms(zs, seg):
        z2 = jnp.concatenate([z * z for z in zs], axis=1).astype(BF16)
        ss = jnp.dot(z2, seg, preferred_element_type=F32)
        return lax.rsqrt(ss * (1.0 / HEAD_DIM) + EPS)

    def rope(z, tab_ref, sh):
        return (z * tab_ref[0] + pltpu.roll(z, LANES - sh, 1) * tab_ref[1]
                + pltpu.roll(z, sh, 1) * tab_ref[2])

    a_sh = HEAD_DIM // 4
    b_sh = PARTIAL_ROPE_DIM // 2
    kv_col = A_Q_W
    b_col = A_Q_W + 2 * A_KV_W
    g_col = b_col + 3 * B_W
    b_refs = (b0_ref, b1_ref, b2_ref)
    dl_slot = [0]

    def epilogue_a_q(col):
        zs = [slab(col)[...], slab(col + LANES)[...]]
        inv = inv_rms(zs, seg_ref[...])
        for i, z in enumerate(zs):
            zn = z * inv[:, i * LANES:(i + 1) * LANES] * qg_ref[...]
            c0 = col + i * LANES
            qa_ref[0, :, c0:c0 + LANES] = rope(zn, ta_ref, a_sh).astype(BF16)

    def epilogue_a_kv(col):
        zk = slab(col)[...]
        zk = rope(zk * inv_rms([zk], seg_ref[:LANES, :LANES]) * kg_ref[...], ta_ref, a_sh)
        zkr = pltpu.roll(zk, HEAD_DIM, 1)
        ka_ref[0, 0] = jnp.where(low, zk, zkr).astype(BF16)
        ka_ref[0, 1] = jnp.where(low, zkr, zk).astype(BF16)
        zv = slab(col + LANES)[...]
        va_ref[0, 0] = jnp.where(low, zv, 1.0).astype(BF16)
        va_ref[0, 1] = jnp.where(low, pltpu.roll(zv, HEAD_DIM, 1), 1.0).astype(BF16)

    def epilogue_b(col):
        t, rem = divmod(col - b_col, B_W)
        g, rem = divmod(rem, B_GROUP_W)
        d = B_DILATIONS[g]
        out_col = t * B_GROUP_W + rem
        tab_ref = (tbq_ref, tbk_ref, None)[t]
        src = slab(col)
        if tab_ref is not None:
            z = rope(src[...], tab_ref, b_sh)
            if d == 1:
                b_refs[g][0, 0, :, out_col:out_col + LANES] = z.astype(BF16)
                return
            src = dl_ref.at[dl_slot[0]]
            dl_slot[0] = 1 - dl_slot[0]
            src[...] = z
        elif d == 1:
            b_refs[g][0, 0, :, out_col:out_col + LANES] = src[...].astype(BF16)
            return
        for r in range(d):
            piece = src[pl.ds(r, tn // d, stride=d), :]
            b_refs[g][0, r, :, out_col:out_col + LANES] = piece.astype(BF16)

    def epilogue(c):
        col = c * PROJ_CHUNK
        while col < (c + 1) * PROJ_CHUNK:
            if col < kv_col:
                epilogue_a_q(col)
                col += 2 * LANES
            elif col < b_col:
                epilogue_a_kv(col)
                col += 2 * LANES
            elif col < g_col:
                epilogue_b(col)
                col += LANES
            else:
                gate = 1.0 / (1.0 + jnp.exp(-slab(col)[...]))
                gate_ref[0, :, col - g_col:col - g_col + LANES] = gate.astype(BF16)
                col += LANES

    stage(order[0])
    for pos, c in enumerate(order):
        if pos + 1 < n_chunk:
            stage(order[pos + 1])
        epilogue(c)


def _inproj(x, w_in, g_mix, q_gain, k_gain, tab_a, tab_bq, tab_bk, *, tn):
    bn, seq, dm = x.shape
    in_w = w_in.shape[1]
    gate_w = in_w - (A_Q_W + 2 * A_KV_W + 3 * B_W)
    grid = (bn, seq // tn)
    tab_spec = pl.BlockSpec((3, tn, LANES), lambda b, i: (0, i, 0))
    head_of = np.arange(2 * LANES) // HEAD_DIM
    seg = jnp.asarray(head_of[:, None] == head_of[None, :], BF16)
    out_shape = (
        jax.ShapeDtypeStruct((bn, seq, A_Q_W), BF16),
        jax.ShapeDtypeStruct((bn, 2 * A_KV_HEADS, seq, LANES), BF16),
    ) + tuple(jax.ShapeDtypeStruct((bn, d, seq // d, 3 * B_GROUP_W), BF16) for d in B_DILATIONS) + (
        jax.ShapeDtypeStruct((bn, seq, gate_w), BF16),
    )
    out_specs = (
        pl.BlockSpec((1, tn, A_Q_W), lambda b, i: (b, i, 0)),
        pl.BlockSpec((1, 2 * A_KV_HEADS, tn, LANES), lambda b, i: (b, 0, i, 0)),
    ) + tuple(pl.BlockSpec((1, d, tn // d, 3 * B_GROUP_W), lambda b, i: (b, 0, i, 0))
              for d in B_DILATIONS) + (
        pl.BlockSpec((1, tn, gate_w), lambda b, i: (b, i, 0)),
    )
    return pl.pallas_call(
        functools.partial(_inproj_kernel, tn=tn),
        grid=grid,
        in_specs=[
            pl.BlockSpec((1, tn, dm), lambda b, i: (b, i, 0)),
            _const_spec((1, dm)),
            _const_spec((dm, in_w)),
            _const_spec((1, LANES)),
            _const_spec((1, LANES)),
            _const_spec(seg.shape),
            tab_spec, tab_spec, tab_spec,
        ],
        out_specs=out_specs,
        out_shape=out_shape,
        scratch_shapes=[pltpu.VMEM((2, PROJ_CHUNK // LANES, tn, LANES), F32),
                        pltpu.VMEM((2, tn, LANES), F32)],
        compiler_params=pltpu.CompilerParams(
            dimension_semantics=("parallel", "parallel"), vmem_limit_bytes=VMEM_LIMIT),
        name="inproj",
    )(x, g_mix, w_in, q_gain, k_gain, seg, tab_a, tab_bq, tab_bk)


def _attn_a_kernel(q_ref, k_ref, v_ref, o_ref, qs_ref, s_ref, pm_ref, m_ref, acc_ref,
                   *, tq, tk, nk):
    group = A_Q_HEADS // A_KV_HEADS
    n_lc = tk // LANES
    lane = lax.broadcasted_iota(jnp.int32, (tq, LANES), 1)
    low = lane < HEAD_DIM
    for h in range(group):
        pair = q_ref[0, :, (h // 2) * LANES:(h // 2 + 1) * LANES].astype(F32)
        keep = low if h % 2 == 0 else jnp.logical_not(low)
        qs_ref[h * tq:(h + 1) * tq, :] = jnp.where(keep, pair, 0.0).astype(BF16)
    m_ref[...] = jnp.full(m_ref.shape, MASK_VALUE, F32)
    acc_ref[...] = jnp.zeros(acc_ref.shape, F32)

    def scores(c):
        k = k_ref[0, 0, c * tk:(c + 1) * tk, :]
        s = lax.dot_general(qs_ref[...], k, (((1,), (1,)), ((), ())),
                            preferred_element_type=F32)
        s_ref[c % 2] = s
        pm = s[:, :LANES]
        for j in range(1, n_lc):
            pm = jnp.maximum(pm, s[:, j * LANES:(j + 1) * LANES])
        pm_ref[c % 2] = pm

    scores(0)
    for c in range(nk):
        if c + 1 < nk:
            scores(c + 1)
        slot = c % 2
        m_prev = m_ref[...]
        m_next = jnp.maximum(m_prev, jnp.max(pm_ref[slot], axis=1, keepdims=True))
        alpha = jnp.exp2(m_prev - m_next)
        m_ref[...] = m_next
        p = jnp.concatenate(
            [jnp.exp2(s_ref[slot, :, j * LANES:(j + 1) * LANES] - m_next) for j in range(n_lc)],
            axis=1).astype(BF16)
        v = v_ref[0, 0, c * tk:(c + 1) * tk, :]
        acc_ref[...] = alpha * acc_ref[...] + jnp.dot(p, v, preferred_element_type=F32)

    acc = acc_ref[...]
    acc_sw = pltpu.roll(acc, HEAD_DIM, 1)
    for pr in range(group // 2):
        ev = slice((2 * pr) * tq, (2 * pr + 1) * tq)
        od = slice((2 * pr + 1) * tq, (2 * pr + 2) * tq)
        out = jnp.where(low, acc[ev] / acc_sw[ev], acc_sw[od] / acc[od])
        o_ref[0, :, pr * LANES:(pr + 1) * LANES] = out.astype(BF16)


def _attn_a(qa, ka, va, *, tq, tk):
    bn, seq, _ = qa.shape
    group = A_Q_HEADS // A_KV_HEADS
    gw = group * HEAD_DIM
    grid = (bn, A_KV_HEADS, seq // tq)
    k_spec = pl.BlockSpec((1, 1, seq, LANES), lambda b, j, i: (b, j, 0, 0))
    v_spec = pl.BlockSpec((1, 1, seq, LANES), lambda b, j, i: (b, A_KV_HEADS + j, 0, 0))
    return pl.pallas_call(
        functools.partial(_attn_a_kernel, tq=tq, tk=tk, nk=seq // tk),
        grid=grid,
        in_specs=[pl.BlockSpec((1, tq, gw), lambda b, j, i: (b, i, j)), k_spec, v_spec],
        out_specs=pl.BlockSpec((1, tq, gw), lambda b, j, i: (b, i, j)),
        out_shape=jax.ShapeDtypeStruct((bn, seq, A_Q_W), BF16),
        scratch_shapes=[
            pltpu.VMEM((group * tq, LANES), BF16),
            pltpu.VMEM((2, group * tq, tk), F32),
            pltpu.VMEM((2, group * tq, LANES), F32),
            pltpu.VMEM((group * tq, LANES), F32),
            pltpu.VMEM((group * tq, LANES), F32),
        ],
        compiler_params=pltpu.CompilerParams(
            dimension_semantics=("parallel", "parallel", "parallel"), vmem_limit_bytes=VMEM_LIMIT),
        name="attn_a",
    )(qa, ka, va)


def _attn_b_kernel(q_ref, k_ref, v_ref, hmask_ref, bias_ref, o_ref, lse_ref,
                   *, sub_len, tq, tk, rb, bb):
    nh = B_HEADS_PER_GROUP
    per_res = sub_len // tq
    n_tiles = bb * rb * per_res
    low = lax.broadcasted_iota(jnp.int32, (tq, LANES), 1) < HEAD_DIM

    def locate(t):
        if isinstance(t, int):
            br, i = divmod(t, per_res)
            b, r = divmod(br, rb)
            return b, r, i * tq, max(0, min(i * tq - B_HALF_WINDOW, sub_len - tk))
        br = t // per_res
        b = br // rb
        i0 = pl.multiple_of((t - br * per_res) * tq, tq)
        w0 = jnp.clip(i0 - B_HALF_WINDOW, 0, sub_len - tk)
        return b, br - b * rb, i0, pl.multiple_of(w0, B_HALF_WINDOW)

    def tile(t):
        b, r, i0, w0 = locate(t)
        q = q_ref[b, r, pl.ds(i0, tq), :]
        k = k_ref[b, r, pl.ds(w0, tk), :]
        v = v_ref[b, r, pl.ds(w0, tk), :]
        qs = jnp.concatenate([q * hmask_ref[h] for h in range(nh)], axis=0)
        s = lax.dot_general(qs, k, (((1,), (1,)), ((), ())), preferred_element_type=F32)
        bias = bias_ref[(i0 - w0) // B_HALF_WINDOW]
        s = s + jnp.concatenate([bias] * nh, axis=0)
        m = jnp.max(s, axis=1, keepdims=True)
        p = jnp.exp2(s - m)
        l = jnp.broadcast_to(jnp.sum(p, axis=1, keepdims=True), (nh * tq, LANES))
        lse = jnp.broadcast_to(m, (nh * tq, LANES)) + jnp.log(l) * LOG2_E
        pv = jnp.dot(p.astype(BF16), v, preferred_element_type=F32)
        for half in range(B_GROUP_W // LANES):
            lanes = slice(half * LANES, (half + 1) * LANES)
            ev = slice((2 * half) * tq, (2 * half + 1) * tq)
            od = slice((2 * half + 1) * tq, (2 * half + 2) * tq)
            num = jnp.where(low, pv[ev, lanes], pv[od, lanes])
            den = jnp.where(low, l[ev], l[od])
            o_ref[b, r, pl.ds(i0, tq), lanes] = (num / den).astype(BF16)
            lse_ref[b, r, pl.ds(i0, tq), lanes] = jnp.where(low, lse[ev], lse[od])

    il = min(B_INTERLEAVE, n_tiles)
    if il == n_tiles:
        for t in range(n_tiles):
            tile(t)
    else:
        def group(g, carry):
            for u in range(il):
                tile(g * il + u)
            return carry

        lax.fori_loop(0, n_tiles // il, group, 0)


def _attn_b(qkv, *, tq=B_Q_TILE, min_tiles=B_MIN_TILES):
    bn, d, sub_len, _ = qkv.shape
    tq = min(tq, sub_len)
    tk = min(tq + 2 * B_HALF_WINDOW, sub_len)
    per_res = sub_len // tq
    rb = min(d, max(1, min_tiles // per_res))
    bb = min(bn, max(1, min_tiles // (rb * per_res)))
    n_tiles = bb * rb * per_res
    assert bn % bb == 0 and d % rb == 0 and n_tiles % min(B_INTERLEAVE, n_tiles) == 0
    lane_head = np.arange(B_GROUP_W) // HEAD_DIM
    hmask = lane_head[None, None, :] == np.arange(B_HEADS_PER_GROUP)[:, None, None]
    hmask = jnp.asarray(np.broadcast_to(hmask, (B_HEADS_PER_GROUP, tq, B_GROUP_W)), BF16)
    rel = np.arange(tq)[:, None] - np.arange(tk)[None, :]
    offs = np.arange(3)[:, None, None] * B_HALF_WINDOW
    bias = jnp.asarray(np.where(np.abs(rel[None] + offs) <= B_HALF_WINDOW, 0.0, MASK_VALUE), F32)

    def spec(c):
        return pl.BlockSpec((bb, rb, sub_len, B_GROUP_W), lambda b, r: (b, r, 0, c))

    out_spec = pl.BlockSpec((bb, rb, sub_len, B_GROUP_W), lambda b, r: (b, r, 0, 0))
    return pl.pallas_call(
        functools.partial(_attn_b_kernel, sub_len=sub_len, tq=tq, tk=tk, rb=rb, bb=bb),
        grid=(bn // bb, d // rb),
        in_specs=[spec(0), spec(1), spec(2), _const_spec(hmask.shape), _const_spec(bias.shape)],
        out_specs=(out_spec, out_spec),
        out_shape=(jax.ShapeDtypeStruct((bn, d, sub_len, B_GROUP_W), BF16),
                   jax.ShapeDtypeStruct((bn, d, sub_len, B_GROUP_W), F32)),
        compiler_params=pltpu.CompilerParams(
            dimension_semantics=("parallel", "parallel"), vmem_limit_bytes=VMEM_LIMIT),
        name="attn_b",
    )(qkv, qkv, qkv, hmask, bias)


def _post_kernel(x_ref, ya_ref, o0_ref, l0_ref, o1_ref, l1_ref, o2_ref, l2_ref, gate_ref,
                 wa_ref, wb_ref, wo_ref, gmlp_ref, wup_ref, wdn_ref, gfin_ref,
                 y_ref, ril_ref, *, tn, ff_chunk, final_norm):
    dm = x_ref.shape[-1]
    n_slab = B_GROUP_W // LANES

    def to_natural_order(src_ref, d, slot):
        for r in range(d):
            blk = src_ref[0, r].astype(F32)
            for s in range(n_slab):
                ril_ref[slot * n_slab + s, pl.ds(r, tn // d, stride=d), :] = blk[:, s * LANES:(s + 1) * LANES]

    to_natural_order(o1_ref, B_DILATIONS[1], 0)
    to_natural_order(l1_ref, B_DILATIONS[1], 1)
    to_natural_order(o2_ref, B_DILATIONS[2], 2)
    to_natural_order(l2_ref, B_DILATIONS[2], 3)

    def natural(slot):
        return jnp.concatenate([ril_ref[slot * n_slab + s] for s in range(n_slab)], axis=1)

    o0 = o0_ref[0, 0].astype(F32)
    l0 = l0_ref[0, 0]
    o1, l1, o2, l2 = (natural(slot) for slot in range(4))
    mx = jnp.maximum(jnp.maximum(l0, l1), l2)
    e0, e1, e2 = jnp.exp2(l0 - mx), jnp.exp2(l1 - mx), jnp.exp2(l2 - mx)
    yb_in = (e0 * o0 + e1 * o1 + e2 * o2) / (e0 + e1 + e2)

    ya = jnp.dot(ya_ref[0], wa_ref[...], preferred_element_type=F32)
    yb = jnp.dot(yb_in.astype(BF16), wb_ref[...], preferred_element_type=F32)
    ga = gate_ref[0, :, :dm].astype(F32)
    gb = gate_ref[0, :, dm:].astype(F32)
    mixed = ya * ga + yb * gb
    x1 = x_ref[0] + jnp.dot(mixed.astype(BF16), wo_ref[...], preferred_element_type=F32)

    ms = jnp.mean(x1 * x1, axis=-1, keepdims=True)
    h2 = (x1 * lax.rsqrt(ms + EPS) * gmlp_ref[...]).astype(BF16)
    acc = x1
    d_ff = wup_ref.shape[1]
    for c in range(d_ff // ff_chunk):
        u = jnp.dot(h2, wup_ref[:, c * ff_chunk:(c + 1) * ff_chunk], preferred_element_type=F32)
        u = jnp.square(jnp.maximum(u, 0.0)).astype(BF16)
        acc = acc + jnp.dot(u, wdn_ref[c * ff_chunk:(c + 1) * ff_chunk, :],
                            preferred_element_type=F32)
    if final_norm:
        ms = jnp.mean(acc * acc, axis=-1, keepdims=True)
        acc = acc * lax.rsqrt(ms + EPS) * gfin_ref[...]
    y_ref[0] = acc


def _post(x, ya, b_outs, gates, w_a_out, w_b_out, w_out, g_mlp, w_up, w_down, g_final,
          *, tn, final_norm, ff_chunk=MLP_CHUNK):
    bn, seq, dm = x.shape
    d_ff = w_up.shape[1]
    grid = (bn, seq // tn)
    in_specs = [
        pl.BlockSpec((1, tn, dm), lambda b, i: (b, i, 0)),
        pl.BlockSpec((1, tn, A_Q_W), lambda b, i: (b, i, 0)),
    ]
    operands = [x, ya]
    for d, (o, lse) in zip(B_DILATIONS, b_outs):
        spec = pl.BlockSpec((1, d, tn // d, B_GROUP_W), lambda b, i: (b, 0, i, 0))
        in_specs += [spec, spec]
        operands += [o, lse]
    in_specs += [
        pl.BlockSpec((1, tn, gates.shape[-1]), lambda b, i: (b, i, 0)),
        _const_spec(w_a_out.shape), _const_spec(w_b_out.shape), _const_spec(w_out.shape),
        _const_spec((1, dm)), _const_spec(w_up.shape), _const_spec(w_down.shape), _const_spec((1, dm)),
    ]
    operands += [gates, w_a_out, w_b_out, w_out, g_mlp, w_up, w_down, g_final]
    n_ril = 4 * (B_GROUP_W // LANES)
    return pl.pallas_call(
        functools.partial(_post_kernel, tn=tn, ff_chunk=ff_chunk, final_norm=final_norm),
        grid=grid,
        in_specs=in_specs,
        out_specs=pl.BlockSpec((1, tn, dm), lambda b, i: (b, i, 0)),
        out_shape=jax.ShapeDtypeStruct((bn, seq, dm), F32),
        scratch_shapes=[pltpu.VMEM((n_ril, tn, LANES), F32)],
        compiler_params=pltpu.CompilerParams(
            dimension_semantics=("parallel", "parallel"), vmem_limit_bytes=VMEM_LIMIT),
        name="post",
    )(*operands)


def _layer(x, w, tabs, *, final_norm, g_final):
    tab_a, tab_bq, tab_bk = tabs
    seq = x.shape[1]
    assert seq % TOKEN_TILE == 0 and seq % A_Q_TILE == 0 and seq % A_K_TILE == 0
    assert TOKEN_TILE % (BF16_ROWS * max(B_DILATIONS)) == 0
    qa, kva, b0, b1, b2, gates = _inproj(
        x, w["w_in"], w["g_mix"], w["q_gain"], w["k_gain"], tab_a, tab_bq, tab_bk, tn=TOKEN_TILE)
    ya = _attn_a(qa, kva, kva, tq=A_Q_TILE, tk=A_K_TILE)
    b_outs = [_attn_b(b) for b in (b0, b1, b2)]
    return _post(x, ya, b_outs, gates, w["w_a_out"], w["w_b_out"], w["w_out"], w["g_mlp"],
                 w["w_up"], w["w_down"], g_final, tn=TOKEN_TILE, final_norm=final_norm)


def _trunk(x, layers, g_final):
    seq = x.shape[1]
    tab_a, tab_b = _rope_tables(seq)
    tabs = tuple(jnp.asarray(t, F32) for t in (tab_a, tab_b * Q_SCALE, tab_b))
    for i, w in enumerate(layers):
        x = _layer(x, w, tabs, final_norm=(i == len(layers) - 1), g_final=g_final)
    return x


def kernel(x_prompt, x_sample, w_in, w_a_out, w_b_out, w_out, g_mix, q_gain, k_gain, g_mlp,
           w_up, w_down, g_final):
    depth = w_in.shape[0]
    reps = LANES // HEAD_DIM
    layers = []
    for l in range(depth):
        layers.append(dict(
            w_in=w_in[l].astype(BF16), w_a_out=w_a_out[l].astype(BF16),
            w_b_out=w_b_out[l].astype(BF16), w_out=w_out[l].astype(BF16),
            w_up=w_up[l].astype(BF16), w_down=w_down[l].astype(BF16),
            g_mix=g_mix[l][None, :], g_mlp=g_mlp[l][None, :],
            q_gain=jnp.tile(q_gain[l] * Q_SCALE, reps)[None, :],
            k_gain=jnp.tile(k_gain[l], reps)[None, :],
        ))
    gf = g_final[None, :]
    return (_trunk(x_prompt, layers, gf), _trunk(x_sample, layers, gf))
```
